```python
import math
import jax, jax.numpy as jnp
from jax import lax
import numpy as np

D_MODEL = 1024
BATCH = 8
SEQ = 4096
DEPTH = 1

D_MIX = D_MODEL
ATTN_HEADS = 8
HEAD_DIM = 64
D_ATTN = ATTN_HEADS * HEAD_DIM
D_CONV = D_MIX - D_ATTN
CONV_GROUPS = 8
CONV_WIDTH = 31
IDX_HEADS = 8
IDX_DIM = 64
TOPK_MAX = 256
Q_BLOCK = 128
ROPE_THETA = 10000.0
PEER_HEADS = 8
PEER_NKEYS = 128
PEER_NEXPERTS = PEER_NKEYS * PEER_NKEYS
PEER_QDIM = 128
PEER_HALF = PEER_QDIM // 2
PEER_TOPK = 16
TOK_BLOCK = 128
LN_EPS = 1e-5
DEEPNORM_ALPHA = (2.0 * DEPTH) ** 0.25
DEEPNORM_BETA = (8.0 * DEPTH) ** -0.25

OFF_Q = 0
OFF_K = OFF_Q + D_ATTN
OFF_V = OFF_K + D_ATTN
OFF_QI = OFF_V + D_ATTN
OFF_KI = OFF_QI + IDX_HEADS * IDX_DIM
OFF_WI = OFF_KI + IDX_DIM
OFF_CONV = OFF_WI + IDX_HEADS
D_IN = OFF_CONV + 2 * D_CONV

kernel_name = "hybrid_dsa_conformer_peer_block"


def layer_norm(x, g, b):
    xf = x.astype(jnp.float32)
    mu = jnp.mean(xf, axis=-1, keepdims=True)
    var = jnp.mean(jnp.square(xf - mu), axis=-1, keepdims=True)
    y = (xf - mu) * lax.rsqrt(var + LN_EPS) * g.astype(jnp.float32) + b.astype(jnp.float32)
    return y.astype(x.dtype)


def rope(x):
    L, D = x.shape[1], x.shape[-1]
    pos = jnp.arange(L, dtype=jnp.float32)
    inv_freq = ROPE_THETA ** (-jnp.arange(0, D, 2, dtype=jnp.float32) / D)
    ang = pos[:, None] * inv_freq[None, :]
    cos = jnp.concatenate([jnp.cos(ang), jnp.cos(ang)], -1)[None, :, None, :]
    sin = jnp.concatenate([jnp.sin(ang), jnp.sin(ang)], -1)[None, :, None, :]
    xf = x.astype(jnp.float32)
    x1, x2 = xf[..., : D // 2], xf[..., D // 2:]
    rot = jnp.concatenate([-x2, x1], axis=-1)
    return (xf * cos + rot * sin).astype(x.dtype)


def dsa_sparse_attention(q, k, v, qi, ki, wi):
    B, L = q.shape[0], q.shape[1]
    n_sel = min(TOPK_MAX, L // 4)
    n_blk = L // Q_BLOCK
    key_pos = jnp.arange(L)
    scale = HEAD_DIM ** -0.5
    gather = jax.vmap(lambda tab, idx: tab[idx])

    def block(bi):
        start = bi * Q_BLOCK
        q_b = lax.dynamic_slice_in_dim(q, start, Q_BLOCK, axis=1)
        qi_b = lax.dynamic_slice_in_dim(qi, start, Q_BLOCK, axis=1)
        wi_b = lax.dynamic_slice_in_dim(wi, start, Q_BLOCK, axis=1)
        q_pos = start + jnp.arange(Q_BLOCK)
        causal = key_pos[None, :] <= q_pos[:, None]
        rel = jax.nn.relu(jnp.einsum('bqhd,bsd->bqhs', qi_b, ki).astype(jnp.float32))
        score = jnp.einsum('bqh,bqhs->bqs', wi_b.astype(jnp.float32), rel)
        score = jnp.where(causal[None], score, -jnp.inf)
        _, sel = lax.top_k(score, n_sel)
        sel_valid = sel <= q_pos[None, :, None]
        k_sel = gather(k, sel)
        v_sel = gather(v, sel)
        att = jnp.einsum('bqhd,bqkhd->bhqk', q_b, k_sel).astype(jnp.float32) * scale
        att = jnp.where(sel_valid[:, None], att, -jnp.inf)
        p = jax.nn.softmax(att, axis=-1).astype(v.dtype)
        return jnp.einsum('bhqk,bqkhd->bqhd', p, v_sel)

    out = lax.map(block, jnp.arange(n_blk))
    out = jnp.transpose(out, (1, 0, 2, 3, 4))
    return out.reshape(B, L, D_ATTN)


def conformer_conv(a, dw_w, dw_b, g, b):
    u, gate = jnp.split(a, 2, axis=-1)
    h = u * jax.nn.sigmoid(gate)
    h = lax.conv_general_dilated(
        h, dw_w[:, None, :].astype(h.dtype), window_strides=(1,),
        padding=[(CONV_WIDTH - 1, 0)],
        dimension_numbers=('NWC', 'WIO', 'NWC'),
        feature_group_count=D_CONV) + dw_b
    h = layer_norm(h, g, b)
    return jax.nn.silu(h)


def peer_ffn(x, w_q, sub1, sub2, u_tab, v_tab):
    B, L, D = x.shape
    xt = x.reshape(-1, TOK_BLOCK, D)

    def block(xb):
        q = (xb @ w_q).reshape(TOK_BLOCK, PEER_HEADS, PEER_QDIM)
        q1, q2 = q[..., :PEER_HALF], q[..., PEER_HALF:]
        s1 = jnp.einsum('thd,hnd->thn', q1, sub1).astype(jnp.float32)
        s2 = jnp.einsum('thd,hnd->thn', q2, sub2).astype(jnp.float32)
        v1, i1 = lax.top_k(s1, PEER_TOPK)
        v2, i2 = lax.top_k(s2, PEER_TOPK)
        cand = (v1[..., :, None] + v2[..., None, :]).reshape(TOK_BLOCK, PEER_HEADS, -1)
        cand_id = (i1[..., :, None] * PEER_NKEYS + i2[..., None, :]).reshape(TOK_BLOCK, PEER_HEADS, -1)
        top_s, pos = lax.top_k(cand, PEER_TOPK)
        expert = jnp.take_along_axis(cand_id, pos, axis=-1)
        g = jax.nn.softmax(top_s, axis=-1).astype(xb.dtype)
        u_e = u_tab[expert]
        v_e = v_tab[expert]
        act = jax.nn.gelu(jnp.einsum('thkd,td->thk', u_e, xb), approximate=False)
        return jnp.einsum('thk,thkd->td', g * act, v_e)

    y = lax.map(block, xt)
    return y.reshape(B, L, D)


def setup_inputs(seed: int = 0) -> dict:
    key = jax.random.key(seed)
    ks = jax.random.split(key, 20)
    f32 = jnp.float32
    nrm = lambda k, shape, s: jax.random.normal(k, shape, f32) * s
    gain = lambda k, shape: 1.0 + 0.02 * jax.random.normal(k, shape, f32)
    return {
        "x": jax.random.normal(ks[0], (BATCH, SEQ, D_MODEL), f32),
        "w_in": nrm(ks[1], (DEPTH, D_MODEL, D_IN), D_MODEL ** -0.5),
        "idx_k_ln_g": gain(ks[2], (DEPTH, IDX_DIM)),
        "idx_k_ln_b": nrm(ks[3], (DEPTH, IDX_DIM), 0.02),
        "conv_dw_w": nrm(ks[4], (DEPTH, CONV_WIDTH, D_CONV), CONV_WIDTH ** -0.5),
        "conv_dw_b": nrm(ks[5], (DEPTH, D_CONV), 0.02),
        "conv_ln_g": gain(ks[6], (DEPTH, D_CONV)),
        "conv_ln_b": nrm(ks[7], (DEPTH, D_CONV), 0.02),
        "w_out": nrm(ks[8], (DEPTH, D_MIX, D_MODEL), D_MIX ** -0.5 * DEEPNORM_BETA),
        "ln1_g": gain(ks[9], (DEPTH, D_MODEL)),
        "ln1_b": nrm(ks[10], (DEPTH, D_MODEL), 0.02),
        "peer_w_q": nrm(ks[11], (DEPTH, D_MODEL, PEER_HEADS * PEER_QDIM), D_MODEL ** -0.5),
        "peer_sub_keys1": nrm(ks[12], (DEPTH, PEER_HEADS, PEER_NKEYS, PEER_HALF), PEER_HALF ** -0.5),
        "peer_sub_keys2": nrm(ks[13], (DEPTH, PEER_HEADS, PEER_NKEYS, PEER_HALF), PEER_HALF ** -0.5),
        "peer_u": nrm(ks[14], (DEPTH, PEER_NEXPERTS, D_MODEL), D_MODEL ** -0.5),
        "peer_v": nrm(ks[15], (DEPTH, PEER_NEXPERTS, D_MODEL), DEEPNORM_BETA),
        "ln2_g": gain(ks[16], (DEPTH, D_MODEL)),
        "ln2_b": nrm(ks[17], (DEPTH, D_MODEL), 0.02),
    }


def reference(x, w_in, idx_k_ln_g, idx_k_ln_b, conv_dw_w, conv_dw_b, conv_ln_g,
              conv_ln_b, w_out, ln1_g, ln1_b, peer_w_q, peer_sub_keys1,
              peer_sub_keys2, peer_u, peer_v, ln2_g, ln2_b):
    B, L, _ = x.shape
    for l in range(DEPTH):
        proj = x @ w_in[l]
        q = rope(proj[..., OFF_Q:OFF_K].reshape(B, L, ATTN_HEADS, HEAD_DIM))
        k = rope(proj[..., OFF_K:OFF_V].reshape(B, L, ATTN_HEADS, HEAD_DIM))
        v = proj[..., OFF_V:OFF_QI].reshape(B, L, ATTN_HEADS, HEAD_DIM)
        qi = rope(proj[..., OFF_QI:OFF_KI].reshape(B, L, IDX_HEADS, IDX_DIM))
        ki = layer_norm(proj[..., OFF_KI:OFF_WI], idx_k_ln_g[l], idx_k_ln_b[l])
        ki = rope(ki[:, :, None, :])[:, :, 0, :]
        wi = proj[..., OFF_WI:OFF_CONV] * (IDX_HEADS ** -0.5)
        attn_out = dsa_sparse_attention(q, k, v, qi, ki, wi)
        conv_out = conformer_conv(proj[..., OFF_CONV:], conv_dw_w[l], conv_dw_b[l],
                                  conv_ln_g[l], conv_ln_b[l])
        mix = jnp.concatenate([attn_out, conv_out], axis=-1) @ w_out[l]
        x = layer_norm(DEEPNORM_ALPHA * x + mix, ln1_g[l], ln1_b[l])
        ffn = peer_ffn(x, peer_w_q[l], peer_sub_keys1[l], peer_sub_keys2[l],
                       peer_u[l], peer_v[l])
        x = layer_norm(DEEPNORM_ALPHA * x + ffn, ln2_g[l], ln2_b[l])
    return x
```

```python
import functools

import jax
import jax.numpy as jnp
from jax import lax
from jax.experimental import pallas as pl
from jax.experimental.pallas import tpu as pltpu

F32 = jnp.float32
BF16 = jnp.bfloat16
I32 = jnp.int32

D_MODEL = 1024
ATTN_HEADS = 8
HEAD_DIM = 64
D_ATTN = ATTN_HEADS * HEAD_DIM
D_CONV = 512
CONV_WIDTH = 31
IDX_HEADS = 8
IDX_DIM = 64
TOPK_MAX = 256
ROPE_THETA = 10000.0
PEER_HEADS = 8
PEER_NKEYS = 128
PEER_NEXPERTS = PEER_NKEYS * PEER_NKEYS
PEER_HALF = 64
PEER_TOPK = 16
LN_EPS = 1e-5
DEPTH = 1
DEEPNORM_ALPHA = (2.0 * DEPTH) ** 0.25

OFF_Q = 0
OFF_K = OFF_Q + D_ATTN
OFF_V = OFF_K + D_ATTN
OFF_QI = OFF_V + D_ATTN
OFF_KI = OFF_QI + IDX_HEADS * IDX_DIM
OFF_WI = OFF_KI + IDX_DIM
OFF_CONV = OFF_WI + IDX_HEADS

LANES = 128
VMEM_LIMIT = 56 * 1024 * 1024

TM = 512
TC = 512
HALO = 32
TQ = 256
TK = 256
TN = 512
TR = 256
TB = 512
EC = 1024

NEG_BIG = -1e30
INT_MIN = -2 ** 31


def _nt_dot(a, b):
    return lax.dot_general(a, b, (((1,), (1,)), ((), ())), preferred_element_type=F32)


def _dot(a, b):
    return jnp.dot(a, b, preferred_element_type=F32)


def _lane_tile(a, n):
    reps = n // LANES
    return a if reps == 1 else jnp.concatenate([a] * reps, axis=1)


def _rope_rows(z, cos, sin):
    x1, x2 = z[:32], z[32:]
    top = x1 * cos[:32] - x2 * sin[:32]
    bot = x2 * cos[32:] + x1 * sin[32:]
    return top, bot


def _proj_kernel(x_ref, wm_ref, ws_ref, wc_ref, cos_ref, sin_ref, kig_ref, kib_ref,
                 qz_ref, qi_ref, v_ref, wi_ref, k_ref, ki_ref, h_ref, kt_scr):
    xb = x_ref[...].astype(BF16)
    cos = cos_ref[...]
    sin = sin_ref[...]
    pT = _nt_dot(wm_ref[...], xb)

    zeros64 = jnp.zeros((HEAD_DIM, TM), BF16)
    for h in range(ATTN_HEADS):
        top, bot = _rope_rows(pT[OFF_Q + h * 64:OFF_Q + (h + 1) * 64], cos, sin)
        qh = (jnp.concatenate([top, bot], axis=0) * (HEAD_DIM ** -0.5)).astype(BF16)
        if h % 2 == 0:
            qz_ref[h * 128:h * 128 + 64, :] = qh
            qz_ref[h * 128 + 64:(h + 1) * 128, :] = zeros64
        else:
            qz_ref[h * 128:h * 128 + 64, :] = zeros64
            qz_ref[h * 128 + 64:(h + 1) * 128, :] = qh
        top, bot = _rope_rows(pT[OFF_K + h * 64:OFF_K + (h + 1) * 64], cos, sin)
        kt_scr[h * 64:h * 64 + 32, :] = top
        kt_scr[h * 64 + 32:(h + 1) * 64, :] = bot
        top, bot = _rope_rows(pT[OFF_QI + h * 64:OFF_QI + (h + 1) * 64], cos, sin)
        qi_ref[h * 64:h * 64 + 32, :] = top.astype(BF16)
        qi_ref[h * 64 + 32:(h + 1) * 64, :] = bot.astype(BF16)
    k_ref[...] = kt_scr[...].T.astype(BF16)

    vT = pT[OFF_V:OFF_V + D_ATTN].astype(BF16)
    for j in range(TM // TK):
        v_ref[j] = vT[:, j * TK:(j + 1) * TK]

    sT = _nt_dot(ws_ref[...], xb)
    z = sT[:IDX_DIM]
    mu = jnp.mean(z, axis=0, keepdims=True)
    var = jnp.mean(jnp.square(z - mu), axis=0, keepdims=True)
    y = (z - mu) * lax.rsqrt(var + LN_EPS) * kig_ref[...] + kib_ref[...]
    top, bot = _rope_rows(y, cos, sin)
    kiT = jnp.concatenate([top, bot, jnp.zeros((LANES - IDX_DIM, TM), F32)], axis=0)
    ki_ref[...] = kiT.T[:, :IDX_DIM].astype(BF16)
    wi_ref[...] = sT[IDX_DIM:IDX_DIM + IDX_HEADS] * (IDX_HEADS ** -0.5)

    a = _dot(xb, wc_ref[...])
    h_ref[...] = a[:, :D_CONV] * jax.nn.sigmoid(a[:, D_CONV:])


def _proj(x, wm, ws, wc, cosT, sinT, kig, kib):
    B, L, _ = x.shape
    grid = (B, L // TM)
    full = lambda shape: pl.BlockSpec(shape, lambda b, i: (0,) * len(shape))
    return pl.pallas_call(
        _proj_kernel,
        grid=grid,
        in_specs=[
            pl.BlockSpec((None, TM, D_MODEL), lambda b, i: (b, i, 0)),
            full(wm.shape), full(ws.shape), full(wc.shape),
            pl.BlockSpec((HEAD_DIM, TM), lambda b, i: (0, i)),
            pl.BlockSpec((HEAD_DIM, TM), lambda b, i: (0, i)),
            full(kig.shape), full(kib.shape),
        ],
        out_specs=[
            pl.BlockSpec((None, ATTN_HEADS * 128, TM), lambda b, i: (b, 0, i)),
            pl.BlockSpec((None, D_ATTN, TM), lambda b, i: (b, 0, i)),
            pl.BlockSpec((None, TM // TK, D_ATTN, TK), lambda b, i: (b, i, 0, 0)),
            pl.BlockSpec((None, IDX_HEADS, TM), lambda b, i: (b, 0, i)),
            pl.BlockSpec((None, TM, D_ATTN), lambda b, i: (b, i, 0)),
            pl.BlockSpec((None, TM, IDX_DIM), lambda b, i: (b, i, 0)),
            pl.BlockSpec((None, TM, D_CONV), lambda b, i: (b, i, 0)),
        ],
        out_shape=[
            jax.ShapeDtypeStruct((B, ATTN_HEADS * 128, L), BF16),
            jax.ShapeDtypeStruct((B, D_ATTN, L), BF16),
            jax.ShapeDtypeStruct((B, L // TK, D_ATTN, TK), BF16),
            jax.ShapeDtypeStruct((B, IDX_HEADS, L), F32),
            jax.ShapeDtypeStruct((B, L, D_ATTN), BF16),
            jax.ShapeDtypeStruct((B, L, IDX_DIM), BF16),
            jax.ShapeDtypeStruct((B, L, D_CONV), F32),
        ],
        scratch_shapes=[pltpu.VMEM((D_ATTN, TM), F32)],
        compiler_params=pltpu.CompilerParams(
            dimension_semantics=("parallel", "parallel"), vmem_limit_bytes=VMEM_LIMIT),
        name="proj",
    )(x, wm, ws, wc, cosT, sinT, kig, kib)


def _conv_kernel(h_ref, halo_ref, w_ref, b_ref, g_ref, beta_ref, o_ref, buf, acc):
    i = pl.program_id(1)
    buf[0:HALO, :] = jnp.where(i > 0, halo_ref[...], 0.0)
    buf[HALO:HALO + TC, :] = h_ref[...]
    base = HALO - (CONV_WIDTH - 1)
    for c in range(D_CONV // LANES):
        cs = slice(c * LANES, (c + 1) * LANES)
        for r in range(TC // LANES):
            a = jnp.zeros((LANES, LANES), F32)
            for j in range(CONV_WIDTH):
                s = base + j + r * LANES
                a = a + buf[s:s + LANES, cs] * w_ref[j:j + 1, cs]
            acc[r * LANES:(r + 1) * LANES, cs] = a
    y = acc[...] + b_ref[...]
    mu = jnp.mean(y, axis=-1, keepdims=True)
    var = jnp.mean(jnp.square(y - mu), axis=-1, keepdims=True)
    y = (y - mu) * lax.rsqrt(var + LN_EPS) * g_ref[...] + beta_ref[...]
    y = y * jax.nn.sigmoid(y)
    o_ref[...] = y.T.astype(BF16)


def _conv(h, w, b, g, beta):
    B, L, _ = h.shape
    full = lambda shape: pl.BlockSpec(shape, lambda b_, i: (0,) * len(shape))
    return pl.pallas_call(
        _conv_kernel,
        grid=(B, L // TC),
        in_specs=[
            pl.BlockSpec((None, TC, D_CONV), lambda b_, i: (b_, i, 0)),
            pl.BlockSpec((None, HALO, D_CONV),
                         lambda b_, i: (b_, jnp.maximum(i * (TC // HALO) - 1, 0), 0)),
            full(w.shape), full(b.shape), full(g.shape), full(beta.shape),
        ],
        out_specs=pl.BlockSpec((None, D_CONV, TC), lambda b_, i: (b_, 0, i)),
        out_shape=jax.ShapeDtypeStruct((B, D_CONV, L), BF16),
        scratch_shapes=[pltpu.VMEM((HALO + TC, D_CONV), F32), pltpu.VMEM((TC, D_CONV), F32)],
        compiler_params=pltpu.CompilerParams(
            dimension_semantics=("parallel", "parallel"), vmem_limit_bytes=VMEM_LIMIT),
        name="conv",
    )(h, h, w, b, g, beta)


def _dsa_kernel(n_sel, qz_ref, qi_ref, wi_ref, k_ref, ki_ref, v_ref, o_ref,
                keys_ref, acc_ref, m_ref, l_ref):
    qb = pl.program_id(1)
    nc = qb + 1
    rows = lax.broadcasted_iota(I32, (TK, TQ), 0)
    cols = lax.broadcasted_iota(I32, (TK, TQ), 1)

    def score_chunk(c, carry):
        kic = ki_ref[pl.ds(pl.multiple_of(c * TK, TK), TK), :]
        s = jnp.zeros((TK, TQ), F32)
        for h in range(IDX_HEADS):
            z = _dot(kic, qi_ref[h * IDX_DIM:(h + 1) * IDX_DIM, :])
            s = s + wi_ref[h:h + 1, :] * jnp.maximum(z, 0.0)
        bits = pltpu.bitcast(s, I32)
        key = jnp.where(bits < 0, bits ^ jnp.int32(0x7FFFFFFF), bits)
        valid = (c * TK + rows) <= (qb * TQ + cols)
        keys_ref[c] = jnp.where(valid, key, jnp.int32(INT_MIN))
        return carry

    lax.fori_loop(0, nc, score_chunk, 0)

    def count_ge(cand):
        def body(c, cnt):
            return cnt + jnp.sum((keys_ref[c] >= cand).astype(I32), axis=0, keepdims=True)
        return lax.fori_loop(0, nc, body, jnp.zeros((1, TQ), I32))

    zero = jnp.zeros((1, TQ), I32)
    prefix = jnp.where(count_ge(zero) >= n_sel, zero, jnp.int32(INT_MIN))

    def bit_step(i, prefix):
        cand = prefix + lax.shift_left(jnp.int32(1), 30 - i)
        return jnp.where(count_ge(cand) >= n_sel, cand, prefix)

    prefix = lax.fori_loop(0, 31, bit_step, prefix)
    thr = jnp.maximum(prefix, jnp.int32(INT_MIN + 1))

    acc_ref[...] = jnp.zeros_like(acc_ref)
    m_ref[...] = jnp.full_like(m_ref, NEG_BIG)
    l_ref[...] = jnp.zeros_like(l_ref)

    def attn_chunk(c, carry):
        sel = keys_ref[c] >= thr
        start = pl.multiple_of(c * TK, TK)
        for h in range(ATTN_HEADS):
            j = h // 2
            kc = k_ref[pl.ds(start, TK), j * 128:(j + 1) * 128]
            att = _dot(kc, qz_ref[h * 128:(h + 1) * 128, :])
            att = jnp.where(sel, att, NEG_BIG)
            m_old = m_ref[h:h + 1, :]
            m_new = jnp.maximum(m_old, jnp.max(att, axis=0, keepdims=True))
            p = jnp.where(sel, jnp.exp(att - m_new), 0.0)
            alpha = jnp.exp(m_old - m_new)
            l_ref[h:h + 1, :] = alpha * l_ref[h:h + 1, :] + jnp.sum(p, axis=0, keepdims=True)
            pv = _dot(v_ref[c, h * 64:(h + 1) * 64, :], p.astype(BF16))
            acc_ref[h * 64:(h + 1) * 64, :] = alpha * acc_ref[h * 64:(h + 1) * 64, :] + pv
            m_ref[h:h + 1, :] = m_new
        return carry

    lax.fori_loop(0, nc, attn_chunk, 0)

    for h in range(ATTN_HEADS):
        o_ref[h * 64:(h + 1) * 64, :] = (
            acc_ref[h * 64:(h + 1) * 64, :] / l_ref[h:h + 1, :]).astype(BF16)


def _dsa(qz, qiT, wiT, k, ki, vTc):
    B, _, L = qiT.shape
    n_sel = min(TOPK_MAX, L // 4)
    nck = L // TK
    return pl.pallas_call(
        functools.partial(_dsa_kernel, n_sel),
        grid=(B, L // TQ),
        in_specs=[
            pl.BlockSpec((None, ATTN_HEADS * 128, TQ), lambda b, q: (b, 0, q)),
            pl.BlockSpec((None, D_ATTN, TQ), lambda b, q: (b, 0, q)),
            pl.BlockSpec((None, IDX_HEADS, TQ), lambda b, q: (b, 0, q)),
            pl.BlockSpec((None, L, D_ATTN), lambda b, q: (b, 0, 0)),
            pl.BlockSpec((None, L, IDX_DIM), lambda b, q: (b, 0, 0)),
            pl.BlockSpec((None, nck, D_ATTN, TK), lambda b, q: (b, 0, 0, 0)),
        ],
        out_specs=pl.BlockSpec((None, D_ATTN, TQ), lambda b, q: (b, 0, q)),
        out_shape=jax.ShapeDtypeStruct((B, D_ATTN, L), BF16),
        scratch_shapes=[
            pltpu.VMEM((nck, TK, TQ), I32),
            pltpu.VMEM((D_ATTN, TQ), F32),
            pltpu.VMEM((ATTN_HEADS, TQ), F32),
            pltpu.VMEM((ATTN_HEADS, TQ), F32),
        ],
        compiler_params=pltpu.CompilerParams(
            dimension_semantics=("parallel", "arbitrary"), vmem_limit_bytes=VMEM_LIMIT),
        name="dsa",
    )(qz, qiT, wiT, k, ki, vTc)


def _ln_rows(y, g, b):
    mu = jnp.mean(y, axis=0, keepdims=True)
    var = jnp.mean(jnp.square(y - mu), axis=0, keepdims=True)
    return (y - mu) * lax.rsqrt(var + LN_EPS) * g + b


def _mix_kernel(a_ref, c_ref, x_ref, wa_ref, wc_ref, g_ref, b_ref, o_ref, ob_ref):
    mix = _dot(wa_ref[...], a_ref[...]) + _dot(wc_ref[...], c_ref[...])
    y = DEEPNORM_ALPHA * x_ref[...].T + mix
    x1 = _ln_rows(y, _lane_tile(g_ref[...], TN), _lane_tile(b_ref[...], TN))
    o_ref[...] = x1
    ob_ref[...] = x1.astype(BF16)


def _mix(attnT, convT, x, woa, woc, g, b):
    B, L, _ = x.shape
    nl = L // TN
    full = lambda shape: pl.BlockSpec(shape, lambda b_, i: (0,) * len(shape))
    return pl.pallas_call(
        _mix_kernel,
        grid=(B, nl),
        in_specs=[
            pl.BlockSpec((None, D_ATTN, TN), lambda b_, i: (b_, 0, i)),
            pl.BlockSpec((None, D_CONV, TN), lambda b_, i: (b_, 0, i)),
            pl.BlockSpec((None, TN, D_MODEL), lambda b_, i: (b_, i, 0)),
            full(woa.shape), full(woc.shape), full(g.shape), full(b.shape),
        ],
        out_specs=[
            pl.BlockSpec((D_MODEL, TN), lambda b_, i: (0, b_ * nl + i)),
            pl.BlockSpec((D_MODEL, TN), lambda b_, i: (0, b_ * nl + i)),
        ],
        out_shape=[
            jax.ShapeDtypeStruct((D_MODEL, B * L), F32),
            jax.ShapeDtypeStruct((D_MODEL, B * L), BF16),
        ],
        compiler_params=pltpu.CompilerParams(
            dimension_semantics=("parallel", "parallel"), vmem_limit_bytes=VMEM_LIMIT),
        name="mix_ln1",
    )(attnT, convT, x, woa, woc, g, b)


def _extract_top(s, n_rows, steps):
    T = s.shape[1]
    iota = lax.broadcasted_iota(I32, (n_rows, T), 0).astype(F32)
    rank = jnp.full((n_rows, T), float(steps), F32)
    vals = []
    for a in range(steps):
        cur = jnp.max(s, axis=0, keepdims=True)
        first = jnp.min(jnp.where(s == cur, iota, float(n_rows)), axis=0, keepdims=True)
        hit = iota == first
        rank = jnp.where(hit, float(a), rank)
        s = jnp.where(hit, -jnp.inf, s)
        vals.append(cur)
    return rank, vals


def _route_kernel(x_ref, wq_ref, s1_ref, s2_ref, r2_ref, c_ref, e1_ref, e2_ref):
    qT = _dot(wq_ref[...], x_ref[...])
    for h in range(PEER_HEADS):
        q1 = qT[h * 128:h * 128 + PEER_HALF].astype(BF16)
        q2 = qT[h * 128 + PEER_HALF:(h + 1) * 128].astype(BF16)
        s1 = _dot(s1_ref[h], q1)
        s2 = _dot(s2_ref[h], q2)
        r1, v1 = _extract_top(s1, PEER_NKEYS, PEER_TOPK)
        r2, v2 = _extract_top(s2, PEER_NKEYS, PEER_TOPK)
        v2m = jnp.concatenate(v2, axis=0)
        cand = jnp.concatenate([v1[a] + v2m for a in range(PEER_TOPK)], axis=0)
        rc, _ = _extract_top(cand, PEER_TOPK * PEER_TOPK, PEER_TOPK)
        chosen = rc < float(PEER_TOPK)
        top = v1[0] + v2[0]
        z = jnp.sum(jnp.where(chosen, jnp.exp(cand - top), 0.0), axis=0, keepdims=True)
        cnt = jnp.zeros((PEER_NKEYS, TR), F32)
        chosen_f = chosen.astype(F32)
        for a in range(PEER_TOPK):
            ca = jnp.sum(chosen_f[a * PEER_TOPK:(a + 1) * PEER_TOPK], axis=0, keepdims=True)
            cnt = jnp.where(r1 == float(a), ca, cnt)
        r2_ref[h] = r2
        c_ref[h] = cnt
        e1_ref[h] = jnp.where(r1 < float(PEER_TOPK), jnp.exp(s1 - v1[0]), 0.0) / z
        e2_ref[h] = jnp.where(r2 < float(PEER_TOPK), jnp.exp(s2 - v2[0]), 0.0)


def _route(x1b, wqT, sub1, sub2):
    _, N = x1b.shape
    full = lambda shape: pl.BlockSpec(shape, lambda i: (0,) * len(shape))
    spec = pl.BlockSpec((PEER_HEADS, PEER_NKEYS, TR), lambda i: (0, 0, i))
    shp = jax.ShapeDtypeStruct((PEER_HEADS, PEER_NKEYS, N), F32)
    return pl.pallas_call(
        _route_kernel,
        grid=(N // TR,),
        in_specs=[pl.BlockSpec((D_MODEL, TR), lambda i: (0, i)),
                  full(wqT.shape), full(sub1.shape), full(sub2.shape)],
        out_specs=[spec, spec, spec, spec],
        out_shape=[shp, shp, shp, shp],
        compiler_params=pltpu.CompilerParams(
            dimension_semantics=("parallel",), vmem_limit_bytes=VMEM_LIMIT),
        name="peer_route",
    )(x1b, wqT, sub1, sub2)


def _peer_kernel(xb_ref, x_ref, r2_ref, c_ref, e1_ref, e2_ref, u_ref, vt_ref, g_ref, b_ref,
                 o_ref, acc_ref):
    c = pl.program_id(1)

    @pl.when(c == 0)
    def _():
        acc_ref[...] = jnp.zeros_like(acc_ref)

    hT = _dot(u_ref[...], xb_ref[...])
    act = 0.5 * hT * (1.0 + lax.erf(hT * (2.0 ** -0.5)))
    gates = []
    for s in range(EC // PEER_NKEYS):
        g = jnp.zeros((PEER_NKEYS, TB), F32)
        for h in range(PEER_HEADS):
            cnt = c_ref[h, s:s + 1, :]
            e1 = e1_ref[h, s:s + 1, :]
            g = g + jnp.where(r2_ref[h] < cnt, e1 * e2_ref[h], 0.0)
        gates.append(g)
    gh = (jnp.concatenate(gates, axis=0) * act).astype(BF16)
    acc_ref[...] += _dot(vt_ref[...], gh)

    @pl.when(c == pl.num_programs(1) - 1)
    def _():
        y = DEEPNORM_ALPHA * x_ref[...] + acc_ref[...]
        x2 = _ln_rows(y, _lane_tile(g_ref[...], TB), _lane_tile(b_ref[...], TB))
        o_ref[...] = x2.T


def _peer(x1b, x1, r2, cnt, e1, e2, u, vt, g, b):
    _, N = x1.shape
    rows = EC // PEER_NKEYS
    full = lambda shape: pl.BlockSpec(shape, lambda j, c: (0,) * len(shape))
    tok = pl.BlockSpec((PEER_HEADS, PEER_NKEYS, TB), lambda j, c: (0, 0, j))
    row = pl.BlockSpec((PEER_HEADS, rows, TB), lambda j, c: (0, c, j))
    return pl.pallas_call(
        _peer_kernel,
        grid=(N // TB, PEER_NEXPERTS // EC),
        in_specs=[
            pl.BlockSpec((D_MODEL, TB), lambda j, c: (0, j)),
            pl.BlockSpec((D_MODEL, TB), lambda j, c: (0, j)),
            tok, row, row, tok,
            pl.BlockSpec((EC, D_MODEL), lambda j, c: (c, 0)),
            pl.BlockSpec((D_MODEL, EC), lambda j, c: (0, c)),
            full(g.shape), full(b.shape),
        ],
        out_specs=pl.BlockSpec((TB, D_MODEL), lambda j, c: (j, 0)),
        out_shape=jax.ShapeDtypeStruct((N, D_MODEL), F32),
        scratch_shapes=[pltpu.VMEM((D_MODEL, TB), F32)],
        compiler_params=pltpu.CompilerParams(
            dimension_semantics=("parallel", "arbitrary"), vmem_limit_bytes=VMEM_LIMIT),
        name="peer_dense",
    )(x1b, x1, r2, cnt, e1, e2, u, vt, g, b)


def _col_bcast(v, n):
    return jnp.broadcast_to(v.astype(F32)[:, None], (v.shape[0], n))


def kernel(x, w_in, idx_k_ln_g, idx_k_ln_b, conv_dw_w, conv_dw_b, conv_ln_g, conv_ln_b,
           w_out, ln1_g, ln1_b, peer_w_q, peer_sub_keys1, peer_sub_keys2, peer_u, peer_v,
           ln2_g, ln2_b):
    B, L, _ = x.shape
    assert L % TM == 0 and L % TQ == 0 and TQ == TK and (B * L) % TB == 0
    l = 0

    w = w_in[l]
    wT = w.T.astype(BF16)
    wm = jnp.concatenate([wT[OFF_Q:OFF_KI]], axis=0)
    ws = jnp.concatenate([wT[OFF_KI:OFF_CONV],
                          jnp.zeros((LANES - IDX_DIM - IDX_HEADS, D_MODEL), BF16)], axis=0)
    wc = w[:, OFF_CONV:].astype(BF16)
    woT = w_out[l].T.astype(BF16)
    woa, woc = woT[:, :D_ATTN], woT[:, D_ATTN:]
    wqT = peer_w_q[l].T.astype(BF16)
    sub1 = peer_sub_keys1[l].astype(BF16)
    sub2 = peer_sub_keys2[l].astype(BF16)
    u = peer_u[l].astype(BF16)
    vt = peer_v[l].T.astype(BF16)

    pos = jnp.arange(L, dtype=F32)
    inv_freq = ROPE_THETA ** (-jnp.arange(0, HEAD_DIM, 2, dtype=F32) / HEAD_DIM)
    ang = inv_freq[:, None] * pos[None, :]
    cosT = jnp.concatenate([jnp.cos(ang), jnp.cos(ang)], axis=0)
    sinT = jnp.concatenate([jnp.sin(ang), jnp.sin(ang)], axis=0)

    qz, qiT, vTc, wiT, k, ki, h = _proj(
        x, wm, ws, wc, cosT, sinT, _col_bcast(idx_k_ln_g[l], TM), _col_bcast(idx_k_ln_b[l], TM))

    dw = jnp.concatenate([conv_dw_w[l], jnp.zeros((1, D_CONV), F32)], axis=0)
    convT = _conv(h, dw, conv_dw_b[l][None, :], conv_ln_g[l][None, :], conv_ln_b[l][None, :])

    attnT = _dsa(qz, qiT, wiT, k, ki, vTc)

    x1, x1b = _mix(attnT, convT, x, woa, woc,
                   _col_bcast(ln1_g[l], LANES), _col_bcast(ln1_b[l], LANES))

    r2, cnt, e1, e2 = _route(x1b, wqT, sub1, sub2)
    out = _peer(x1b, x1, r2, cnt, e1, e2, u, vt,
                _col_bcast(ln2_g[l], LANES), _col_bcast(ln2_b[l], LANES))
    return out.reshape(B, L, D_MODEL)
```

```python
import functools

import jax
import jax.numpy as jnp
from jax import lax
from jax.experimental import pallas as pl
from jax.experimental.pallas import tpu as pltpu

F32 = jnp.float32
BF16 = jnp.bfloat16
I32 = jnp.int32

D_MODEL = 1024
ATTN_HEADS = 8
HEAD_DIM = 64
D_ATTN = ATTN_HEADS * HEAD_DIM
D_CONV = 512
CONV_WIDTH = 31
IDX_HEADS = 8
IDX_DIM = 64
TOPK_MAX = 256
ROPE_THETA = 10000.0
PEER_HEADS = 8
PEER_NKEYS = 128
PEER_NEXPERTS = PEER_NKEYS * PEER_NKEYS
PEER_HALF = 64
PEER_TOPK = 16
LN_EPS = 1e-5
DEPTH = 1
DEEPNORM_ALPHA = (2.0 * DEPTH) ** 0.25

OFF_Q = 0
OFF_K = OFF_Q + D_ATTN
OFF_V = OFF_K + D_ATTN
OFF_QI = OFF_V + D_ATTN
OFF_KI = OFF_QI + IDX_HEADS * IDX_DIM
OFF_WI = OFF_KI + IDX_DIM
OFF_CONV = OFF_WI + IDX_HEADS

LANES = 128
PACK = 16
VMEM_LIMIT = 56 * 1024 * 1024

TM = 512
TC = 512
HALO = 32
TQ = 256
TK = 256
TN = 512
TR = 256
TB = 512
EC = 1024
PIECE = 256

LOG2E = 1.4426950408889634
Q_SCALE = HEAD_DIM ** -0.5 * LOG2E
NEG_BIG = -1e30
INT_MIN = -2 ** 31


def _nt_dot(a, b):
    return lax.dot_general(a, b, (((1,), (1,)), ((), ())), preferred_element_type=F32)


def _dot(a, b):
    return jnp.dot(a, b, preferred_element_type=F32)


def _lane_tile(a, n):
    reps = n // LANES
    return a if reps == 1 else jnp.concatenate([a] * reps, axis=1)


def _rope_rows(z, cos, sin):
    x1, x2 = z[:32], z[32:]
    top = x1 * cos[:32] - x2 * sin[:32]
    bot = x2 * cos[32:] + x1 * sin[32:]
    return top, bot


def _proj_kernel(x_ref, wm_ref, ws_ref, wc_ref, cos_ref, sin_ref, kig_ref, kib_ref,
                 qz_ref, qi_ref, v_ref, wi_ref, k_ref, ki_ref, h_ref, kt_scr):
    xb = x_ref[...].astype(BF16)
    cos = cos_ref[...]
    sin = sin_ref[...]
    pT = _nt_dot(wm_ref[...], xb)

    zeros64 = jnp.zeros((HEAD_DIM, TM), BF16)
    for h in range(ATTN_HEADS):
        top, bot = _rope_rows(pT[OFF_Q + h * 64:OFF_Q + (h + 1) * 64], cos, sin)
        qh = (jnp.concatenate([top, bot], axis=0) * Q_SCALE).astype(BF16)
        if h % 2 == 0:
            qz_ref[h * 128:h * 128 + 64, :] = qh
            qz_ref[h * 128 + 64:(h + 1) * 128, :] = zeros64
        else:
            qz_ref[h * 128:h * 128 + 64, :] = zeros64
            qz_ref[h * 128 + 64:(h + 1) * 128, :] = qh
        top, bot = _rope_rows(pT[OFF_K + h * 64:OFF_K + (h + 1) * 64], cos, sin)
        kt_scr[h * 64:h * 64 + 32, :] = top
        kt_scr[h * 64 + 32:(h + 1) * 64, :] = bot
        top, bot = _rope_rows(pT[OFF_QI + h * 64:OFF_QI + (h + 1) * 64], cos, sin)
        qi_ref[h * 64:h * 64 + 32, :] = top.astype(BF16)
        qi_ref[h * 64 + 32:(h + 1) * 64, :] = bot.astype(BF16)
    k_ref[...] = kt_scr[...].T.astype(BF16)

    vT = pT[OFF_V:OFF_V + D_ATTN].astype(BF16)
    for j in range(TM // TK):
        v_ref[j] = vT[:, j * TK:(j + 1) * TK]

    sT = _nt_dot(ws_ref[...], xb)
    z = sT[:IDX_DIM]
    mu = jnp.mean(z, axis=0, keepdims=True)
    var = jnp.mean(jnp.square(z - mu), axis=0, keepdims=True)
    y = (z - mu) * lax.rsqrt(var + LN_EPS) * kig_ref[...] + kib_ref[...]
    top, bot = _rope_rows(y, cos, sin)
    kiT = jnp.concatenate([top, bot, jnp.zeros((LANES - IDX_DIM, TM), F32)], axis=0)
    ki_ref[...] = kiT.T[:, :IDX_DIM].astype(BF16)
    wi_ref[...] = sT[IDX_DIM:IDX_DIM + IDX_HEADS] * (IDX_HEADS ** -0.5)

    a = _dot(xb, wc_ref[...])
    h_ref[...] = a[:, :D_CONV] * jax.nn.sigmoid(a[:, D_CONV:])


def _proj(x, wm, ws, wc, cosT, sinT, kig, kib):
    B, L, _ = x.shape
    grid = (B, L // TM)
    full = lambda shape: pl.BlockSpec(shape, lambda b, i: (0,) * len(shape))
    return pl.pallas_call(
        _proj_kernel,
        grid=grid,
        in_specs=[
            pl.BlockSpec((None, TM, D_MODEL), lambda b, i: (b, i, 0)),
            full(wm.shape), full(ws.shape), full(wc.shape),
            pl.BlockSpec((HEAD_DIM, TM), lambda b, i: (0, i)),
            pl.BlockSpec((HEAD_DIM, TM), lambda b, i: (0, i)),
            full(kig.shape), full(kib.shape),
        ],
        out_specs=[
            pl.BlockSpec((None, ATTN_HEADS * 128, TM), lambda b, i: (b, 0, i)),
            pl.BlockSpec((None, D_ATTN, TM), lambda b, i: (b, 0, i)),
            pl.BlockSpec((None, TM // TK, D_ATTN, TK), lambda b, i: (b, i, 0, 0)),
            pl.BlockSpec((None, IDX_HEADS, TM), lambda b, i: (b, 0, i)),
            pl.BlockSpec((None, TM, D_ATTN), lambda b, i: (b, i, 0)),
            pl.BlockSpec((None, TM, IDX_DIM), lambda b, i: (b, i, 0)),
            pl.BlockSpec((None, TM, D_CONV), lambda b, i: (b, i, 0)),
        ],
        out_shape=[
            jax.ShapeDtypeStruct((B, ATTN_HEADS * 128, L), BF16),
            jax.ShapeDtypeStruct((B, D_ATTN, L), BF16),
            jax.ShapeDtypeStruct((B, L // TK, D_ATTN, TK), BF16),
            jax.ShapeDtypeStruct((B, IDX_HEADS, L), F32),
            jax.ShapeDtypeStruct((B, L, D_ATTN), BF16),
            jax.ShapeDtypeStruct((B, L, IDX_DIM), BF16),
            jax.ShapeDtypeStruct((B, L, D_CONV), F32),
        ],
        scratch_shapes=[pltpu.VMEM((D_ATTN, TM), F32)],
        compiler_params=pltpu.CompilerParams(
            dimension_semantics=("parallel", "parallel"), vmem_limit_bytes=VMEM_LIMIT),
        name="proj",
    )(x, wm, ws, wc, cosT, sinT, kig, kib)


def _conv_kernel(h_ref, halo_ref, w_ref, b_ref, g_ref, beta_ref, o_ref, buf, acc):
    i = pl.program_id(1)
    buf[0:HALO, :] = jnp.where(i > 0, halo_ref[...], 0.0)
    buf[HALO:HALO + TC, :] = h_ref[...]
    base = HALO - (CONV_WIDTH - 1)
    for c in range(D_CONV // LANES):
        cs = slice(c * LANES, (c + 1) * LANES)
        for r in range(TC // LANES):
            a = jnp.zeros((LANES, LANES), F32)
            for j in range(CONV_WIDTH):
                s = base + j + r * LANES
                a = a + buf[s:s + LANES, cs] * w_ref[j:j + 1, cs]
            acc[r * LANES:(r + 1) * LANES, cs] = a
    y = acc[...] + b_ref[...]
    mu = jnp.mean(y, axis=-1, keepdims=True)
    var = jnp.mean(jnp.square(y - mu), axis=-1, keepdims=True)
    y = (y - mu) * lax.rsqrt(var + LN_EPS) * g_ref[...] + beta_ref[...]
    y = y * jax.nn.sigmoid(y)
    o_ref[...] = y.T.astype(BF16)


def _conv(h, w, b, g, beta):
    B, L, _ = h.shape
    full = lambda shape: pl.BlockSpec(shape, lambda b_, i: (0,) * len(shape))
    return pl.pallas_call(
        _conv_kernel,
        grid=(B, L // TC),
        in_specs=[
            pl.BlockSpec((None, TC, D_CONV), lambda b_, i: (b_, i, 0)),
            pl.BlockSpec((None, HALO, D_CONV),
                         lambda b_, i: (b_, jnp.maximum(i * (TC // HALO) - 1, 0), 0)),
            full(w.shape), full(b.shape), full(g.shape), full(beta.shape),
        ],
        out_specs=pl.BlockSpec((None, D_CONV, TC), lambda b_, i: (b_, 0, i)),
        out_shape=jax.ShapeDtypeStruct((B, D_CONV, L), BF16),
        scratch_shapes=[pltpu.VMEM((HALO + TC, D_CONV), F32), pltpu.VMEM((TC, D_CONV), F32)],
        compiler_params=pltpu.CompilerParams(
            dimension_semantics=("parallel", "parallel"), vmem_limit_bytes=VMEM_LIMIT),
        name="conv",
    )(h, h, w, b, g, beta)


def _dsa_kernel(n_sel, qz_ref, qi_ref, wi_ref, k_ref, ki_ref, v_ref, o_ref,
                keys_ref, acc_ref, m_ref, l_ref, bias_ref, att_ref):
    qb = pl.program_id(1)
    nc = qb + 1
    rows = lax.broadcasted_iota(I32, (TK, TQ), 0)
    cols = lax.broadcasted_iota(I32, (TK, TQ), 1)

    def score_chunk(c, carry):
        kic = ki_ref[pl.ds(pl.multiple_of(c * TK, TK), TK), :]
        s = jnp.zeros((TK, TQ), F32)
        for h in range(IDX_HEADS):
            z = _dot(kic, qi_ref[h * IDX_DIM:(h + 1) * IDX_DIM, :])
            s = s + wi_ref[h:h + 1, :] * jnp.maximum(z, 0.0)
        bits = pltpu.bitcast(s, I32)
        key = jnp.where(bits < 0, bits ^ jnp.int32(0x7FFFFFFF), bits)
        valid = (c * TK + rows) <= (qb * TQ + cols)
        keys_ref[c] = jnp.where(valid, key, jnp.int32(INT_MIN))
        return carry

    lax.fori_loop(0, nc, score_chunk, 0)

    def count_ge(cand):
        def body(c, cnt):
            return cnt + jnp.sum((keys_ref[c] >= cand).astype(I32), axis=0, keepdims=True)
        return lax.fori_loop(0, nc, body, jnp.zeros((1, TQ), I32))

    zero = jnp.zeros((1, TQ), I32)
    prefix = jnp.where(count_ge(zero) >= n_sel, zero, jnp.int32(INT_MIN))

    def bit_step(i, prefix):
        cand = prefix + lax.shift_left(jnp.int32(1), 30 - i)
        return jnp.where(count_ge(cand) >= n_sel, cand, prefix)

    prefix = lax.fori_loop(0, 31, bit_step, prefix)
    thr = jnp.maximum(prefix, jnp.int32(INT_MIN + 1))

    acc_ref[...] = jnp.zeros_like(acc_ref)
    m_ref[...] = jnp.full_like(m_ref, NEG_BIG)
    l_ref[...] = jnp.zeros_like(l_ref)

    def attn_chunk(c, carry):
        bias_ref[...] = jnp.where(keys_ref[c] >= thr, 0.0, NEG_BIG)
        start = pl.multiple_of(c * TK, TK)

        def logits(h):
            kc = k_ref[pl.ds(start, TK), (h // 2) * 128:(h // 2 + 1) * 128]
            att_ref[h] = _dot(kc, qz_ref[h * 128:(h + 1) * 128, :]) + bias_ref[...]

        logits(0)
        for h in range(ATTN_HEADS):
            if h + 1 < ATTN_HEADS:
                logits(h + 1)
            m_old = m_ref[h:h + 1, :]
            m_new = jnp.maximum(m_old, jnp.max(att_ref[h], axis=0, keepdims=True))
            p = jnp.exp2(att_ref[h] - m_new)
            alpha = jnp.exp2(m_old - m_new)
            l_ref[h:h + 1, :] = alpha * l_ref[h:h + 1, :] + jnp.sum(p, axis=0, keepdims=True)
            pv = _dot(v_ref[c, h * 64:(h + 1) * 64, :], p.astype(BF16))
            acc_ref[h * 64:(h + 1) * 64, :] = alpha * acc_ref[h * 64:(h + 1) * 64, :] + pv
            m_ref[h:h + 1, :] = m_new
        return carry

    lax.fori_loop(0, nc, attn_chunk, 0)

    for h in range(ATTN_HEADS):
        o_ref[h * 64:(h + 1) * 64, :] = (
            acc_ref[h * 64:(h + 1) * 64, :] / l_ref[h:h + 1, :]).astype(BF16)


def _dsa(qz, qiT, wiT, k, ki, vTc):
    B, _, L = qiT.shape
    n_sel = min(TOPK_MAX, L // 4)
    nck = L // TK
    return pl.pallas_call(
        functools.partial(_dsa_kernel, n_sel),
        grid=(B, L // TQ),
        in_specs=[
            pl.BlockSpec((None, ATTN_HEADS * 128, TQ), lambda b, q: (b, 0, q)),
            pl.BlockSpec((None, D_ATTN, TQ), lambda b, q: (b, 0, q)),
            pl.BlockSpec((None, IDX_HEADS, TQ), lambda b, q: (b, 0, q)),
            pl.BlockSpec((None, L, D_ATTN), lambda b, q: (b, 0, 0)),
            pl.BlockSpec((None, L, IDX_DIM), lambda b, q: (b, 0, 0)),
            pl.BlockSpec((None, nck, D_ATTN, TK), lambda b, q: (b, 0, 0, 0)),
        ],
        out_specs=pl.BlockSpec((None, D_ATTN, TQ), lambda b, q: (b, 0, q)),
        out_shape=jax.ShapeDtypeStruct((B, D_ATTN, L), BF16),
        scratch_shapes=[
            pltpu.VMEM((nck, TK, TQ), I32),
            pltpu.VMEM((D_ATTN, TQ), F32),
            pltpu.VMEM((ATTN_HEADS, TQ), F32),
            pltpu.VMEM((ATTN_HEADS, TQ), F32),
            pltpu.VMEM((TK, TQ), F32),
            pltpu.VMEM((ATTN_HEADS, TK, TQ), F32),
        ],
        compiler_params=pltpu.CompilerParams(
            dimension_semantics=("parallel", "arbitrary"), vmem_limit_bytes=VMEM_LIMIT),
        name="dsa",
    )(qz, qiT, wiT, k, ki, vTc)


def _ln_rows(y, g, b):
    mu = jnp.mean(y, axis=0, keepdims=True)
    var = jnp.mean(jnp.square(y - mu), axis=0, keepdims=True)
    return (y - mu) * lax.rsqrt(var + LN_EPS) * g + b


def _mix_kernel(a_ref, c_ref, x_ref, wa_ref, wc_ref, g_ref, b_ref, o_ref, ob_ref):
    mix = _dot(wa_ref[...], a_ref[...]) + _dot(wc_ref[...], c_ref[...])
    y = DEEPNORM_ALPHA * x_ref[...].T + mix
    x1 = _ln_rows(y, _lane_tile(g_ref[...], TN), _lane_tile(b_ref[...], TN))
    o_ref[...] = x1
    ob_ref[...] = x1.astype(BF16)


def _mix(attnT, convT, x, woa, woc, g, b):
    B, L, _ = x.shape
    nl = L // TN
    full = lambda shape: pl.BlockSpec(shape, lambda b_, i: (0,) * len(shape))
    return pl.pallas_call(
        _mix_kernel,
        grid=(B, nl),
        in_specs=[
            pl.BlockSpec((None, D_ATTN, TN), lambda b_, i: (b_, 0, i)),
            pl.BlockSpec((None, D_CONV, TN), lambda b_, i: (b_, 0, i)),
            pl.BlockSpec((None, TN, D_MODEL), lambda b_, i: (b_, i, 0)),
            full(woa.shape), full(woc.shape), full(g.shape), full(b.shape),
        ],
        out_specs=[
            pl.BlockSpec((D_MODEL, TN), lambda b_, i: (0, b_ * nl + i)),
            pl.BlockSpec((D_MODEL, TN), lambda b_, i: (0, b_ * nl + i)),
        ],
        out_shape=[
            jax.ShapeDtypeStruct((D_MODEL, B * L), F32),
            jax.ShapeDtypeStruct((D_MODEL, B * L), BF16),
        ],
        compiler_params=pltpu.CompilerParams(
            dimension_semantics=("parallel", "parallel"), vmem_limit_bytes=VMEM_LIMIT),
        name="mix_ln1",
    )(attnT, convT, x, woa, woc, g, b)


def _extract_top(s, n_rows, steps):
    T = s.shape[1]
    iota = lax.broadcasted_iota(I32, (n_rows, T), 0).astype(F32)
    rank = jnp.full((n_rows, T), float(steps), F32)
    vals = []
    for a in range(steps):
        cur = jnp.max(s, axis=0, keepdims=True)
        first = jnp.min(jnp.where(s == cur, iota, float(n_rows)), axis=0, keepdims=True)
        hit = iota == first
        rank = jnp.where(hit, float(a), rank)
        s = jnp.where(hit, -jnp.inf, s)
        vals.append(cur)
    return rank, vals


def _route_kernel(x_ref, wq_ref, s1_ref, s2_ref, r2_ref, c_ref, e1_ref, e2_ref):
    qT = _dot(wq_ref[...], x_ref[...])
    neg_inf = jnp.full((8, TR), -jnp.inf, F32)
    sub8 = lax.broadcasted_iota(I32, (8, TR), 0)
    for h in range(PEER_HEADS):
        q1 = qT[h * 128:h * 128 + PEER_HALF].astype(BF16)
        q2 = qT[h * 128 + PEER_HALF:(h + 1) * 128].astype(BF16)
        s1 = _dot(s1_ref[h], q1)
        s2 = _dot(s2_ref[h], q2)
        r1, v1 = _extract_top(s1, PEER_NKEYS, PEER_TOPK)
        r2, v2 = _extract_top(s2, PEER_NKEYS, PEER_TOPK)
        v1m = jnp.concatenate(v1, axis=0)
        v2m = jnp.concatenate(v2, axis=0)
        blocks = [v1[0] + v2m[:8], v1[0] + v2m[8:]]
        for a in range(1, 8):
            blocks.append(jnp.where(sub8 < PEER_TOPK // (a + 1), v1[a] + v2m[:8], neg_inf))
        blocks.append(v1m[8:] + v2[0])
        cand = jnp.concatenate(blocks, axis=0)
        rc, _ = _extract_top(cand, cand.shape[0], PEER_TOPK)
        chosen = rc < float(PEER_TOPK)
        top = v1[0] + v2[0]
        z = jnp.sum(jnp.where(chosen, jnp.exp(cand - top), 0.0), axis=0, keepdims=True)
        chosen_f = chosen.astype(F32)
        per_a = [jnp.sum(chosen_f[:16], axis=0, keepdims=True)]
        for a in range(1, 8):
            per_a.append(jnp.sum(chosen_f[8 + 8 * a:16 + 8 * a], axis=0, keepdims=True))
        for a in range(8, PEER_TOPK):
            per_a.append(chosen_f[64 + a:65 + a])
        cnt = jnp.zeros((PEER_NKEYS, TR), F32)
        for a in range(PEER_TOPK):
            cnt = jnp.where(r1 == float(a), per_a[a], cnt)
        r2_ref[h] = r2.astype(BF16)
        c_ref[h] = cnt
        e1_ref[h] = jnp.where(r1 < float(PEER_TOPK), jnp.exp(s1 - v1[0]), 0.0) / z
        e2_ref[h] = jnp.where(r2 < float(PEER_TOPK), jnp.exp(s2 - v2[0]), 0.0).astype(BF16)


def _route(x1b, wqT, sub1, sub2):
    _, N = x1b.shape
    full = lambda shape: pl.BlockSpec(shape, lambda i: (0,) * len(shape))
    spec = pl.BlockSpec((PEER_HEADS, PEER_NKEYS, TR), lambda i: (0, 0, i))
    f32 = jax.ShapeDtypeStruct((PEER_HEADS, PEER_NKEYS, N), F32)
    b16 = jax.ShapeDtypeStruct((PEER_HEADS, PEER_NKEYS, N), BF16)
    return pl.pallas_call(
        _route_kernel,
        grid=(N // TR,),
        in_specs=[pl.BlockSpec((D_MODEL, TR), lambda i: (0, i)),
                  full(wqT.shape), full(sub1.shape), full(sub2.shape)],
        out_specs=[spec, spec, spec, spec],
        out_shape=[b16, f32, f32, b16],
        compiler_params=pltpu.CompilerParams(
            dimension_semantics=("parallel",), vmem_limit_bytes=VMEM_LIMIT),
        name="peer_route",
    )(x1b, wqT, sub1, sub2)


def _peer_kernel(xb_ref, x_ref, r2_ref, c_ref, e1_ref, e2_ref, u_ref, vt_ref, g_ref, b_ref,
                 o_ref, acc_ref, hh_ref, gh_ref):
    c = pl.program_id(1)

    @pl.when(c == 0)
    def _():
        acc_ref[...] = jnp.zeros_like(acc_ref)

    def packed_rows(row):
        bits = pltpu.bitcast(row.astype(BF16).astype(F32), I32)
        both = bits | lax.shift_right_logical(bits, 16)
        return pltpu.bitcast(jnp.broadcast_to(both, (PEER_NKEYS // 2, TB)), BF16)

    zero = jnp.zeros((), BF16)

    def gates(s):
        g = None
        for h in range(PEER_HEADS):
            cb = packed_rows(c_ref[h, s:s + 1, :])
            eb = packed_rows(e1_ref[h, s:s + 1, :])
            t = jnp.where(r2_ref[h] < cb, e2_ref[h], zero) * eb
            g = t if g is None else g + t
        gh_ref[s * PEER_NKEYS:(s + 1) * PEER_NKEYS, :] = g

    def up(k):
        ps = slice(k * PIECE, (k + 1) * PIECE)
        hh_ref[ps, :] = _dot(u_ref[ps, :], xb_ref[...])

    def activate(k):
        ps = slice(k * PIECE, (k + 1) * PIECE)
        hh = hh_ref[ps, :]
        act = (hh * (1.0 + lax.erf(hh * (2.0 ** 0.5)))).astype(BF16)
        gh_ref[ps, :] = gh_ref[ps, :] * act

    def down(k0, k1):
        ps = slice(k0 * PIECE, k1 * PIECE)
        acc_ref[...] += _dot(vt_ref[:, ps], gh_ref[ps, :])

    n_piece = EC // PIECE
    per_piece = PIECE // PEER_NKEYS
    up(0)
    for k in range(n_piece):
        if k + 1 < n_piece:
            up(k + 1)
        for s in range(k * per_piece, (k + 1) * per_piece):
            gates(s)
        activate(k)
        if k % 2 == 1:
            down(k - 1, k + 1)

    @pl.when(c == pl.num_programs(1) - 1)
    def _():
        y = DEEPNORM_ALPHA * x_ref[...] + acc_ref[...]
        x2 = _ln_rows(y, _lane_tile(g_ref[...], TB), _lane_tile(b_ref[...], TB))
        o_ref[...] = x2.T


def _peer(x1b, x1, r2, cnt, e1, e2, u, vt, g, b):
    _, N = x1.shape
    rows = EC // PEER_NKEYS
    full = lambda shape: pl.BlockSpec(shape, lambda j, c: (0,) * len(shape))
    tok = pl.BlockSpec((PEER_HEADS, PEER_NKEYS, TB), lambda j, c: (0, 0, j))
    row = pl.BlockSpec((PEER_HEADS, rows, TB), lambda j, c: (0, c, j))
    return pl.pallas_call(
        _peer_kernel,
        grid=(N // TB, PEER_NEXPERTS // EC),
        in_specs=[
            pl.BlockSpec((D_MODEL, TB), lambda j, c: (0, j)),
            pl.BlockSpec((D_MODEL, TB), lambda j, c: (0, j)),
            tok, row, row, tok,
            pl.BlockSpec((EC, D_MODEL), lambda j, c: (c, 0)),
            pl.BlockSpec((D_MODEL, EC), lambda j, c: (0, c)),
            full(g.shape), full(b.shape),
        ],
        out_specs=pl.BlockSpec((TB, D_MODEL), lambda j, c: (j, 0)),
        out_shape=jax.ShapeDtypeStruct((N, D_MODEL), F32),
        scratch_shapes=[pltpu.VMEM((D_MODEL, TB), F32),
                        pltpu.VMEM((EC, TB), F32),
                        pltpu.VMEM((EC, TB), BF16)],
        compiler_params=pltpu.CompilerParams(
            dimension_semantics=("parallel", "arbitrary"), vmem_limit_bytes=VMEM_LIMIT),
        name="peer_dense",
    )(x1b, x1, r2, cnt, e1, e2, u, vt, g, b)


def _col_bcast(v, n):
    return jnp.broadcast_to(v.astype(F32)[:, None], (v.shape[0], n))


def kernel(x, w_in, idx_k_ln_g, idx_k_ln_b, conv_dw_w, conv_dw_b, conv_ln_g, conv_ln_b,
           w_out, ln1_g, ln1_b, peer_w_q, peer_sub_keys1, peer_sub_keys2, peer_u, peer_v,
           ln2_g, ln2_b):
    B, L, _ = x.shape
    assert L % TM == 0 and L % TQ == 0 and TQ == TK and (B * L) % TB == 0
    l = 0

    w = w_in[l]
    wT = w.T.astype(BF16)
    wm = jnp.concatenate([wT[OFF_Q:OFF_KI]], axis=0)
    ws = jnp.concatenate([wT[OFF_KI:OFF_CONV],
                          jnp.zeros((LANES - IDX_DIM - IDX_HEADS, D_MODEL), BF16)], axis=0)
    wc = w[:, OFF_CONV:].astype(BF16)
    woT = w_out[l].T.astype(BF16)
    woa, woc = woT[:, :D_ATTN], woT[:, D_ATTN:]
    wqT = peer_w_q[l].T.astype(BF16)
    sub1 = peer_sub_keys1[l].astype(BF16)
    sub2 = peer_sub_keys2[l].astype(BF16)
    u = (0.5 * peer_u[l]).astype(BF16)
    vt = peer_v[l].T.astype(BF16)

    pos = jnp.arange(L, dtype=F32)
    inv_freq = ROPE_THETA ** (-jnp.arange(0, HEAD_DIM, 2, dtype=F32) / HEAD_DIM)
    ang = inv_freq[:, None] * pos[None, :]
    cosT = jnp.concatenate([jnp.cos(ang), jnp.cos(ang)], axis=0)
    sinT = jnp.concatenate([jnp.sin(ang), jnp.sin(ang)], axis=0)

    qz, qiT, vTc, wiT, k, ki, h = _proj(
        x, wm, ws, wc, cosT, sinT, _col_bcast(idx_k_ln_g[l], TM), _col_bcast(idx_k_ln_b[l], TM))

    dw = jnp.concatenate([conv_dw_w[l], jnp.zeros((1, D_CONV), F32)], axis=0)
    convT = _conv(h, dw, conv_dw_b[l][None, :], conv_ln_g[l][None, :], conv_ln_b[l][None, :])

    attnT = _dsa(qz, qiT, wiT, k, ki, vTc)

    x1, x1b = _mix(attnT, convT, x, woa, woc,
                   _col_bcast(ln1_g[l], LANES), _col_bcast(ln1_b[l], LANES))

    r2, cnt, e1, e2 = _route(x1b, wqT, sub1, sub2)
    out = _peer(x1b, x1, r2, cnt, e1, e2, u, vt,
                _col_bcast(ln2_g[l], LANES), _col_bcast(ln2_b[l], LANES))
    return out.reshape(B, L, D_MODEL)
```

```python
import functools

import jax
import jax.numpy as jnp
from jax import lax
from jax.experimental import pallas as pl
from jax.experimental.pallas import tpu as pltpu

F32 = jnp.float32
BF16 = jnp.bfloat16
I32 = jnp.int32

D_MODEL = 1024
ATTN_HEADS = 8
HEAD_DIM = 64
D_ATTN = ATTN_HEADS * HEAD_DIM
D_CONV = 512
CONV_WIDTH = 31
IDX_HEADS = 8
IDX_DIM = 64
TOPK_MAX = 256
ROPE_THETA = 10000.0
PEER_HEADS = 8
PEER_NKEYS = 128
PEER_NEXPERTS = PEER_NKEYS * PEER_NKEYS
PEER_HALF = 64
PEER_TOPK = 16
LN_EPS = 1e-5
DEPTH = 1
DEEPNORM_ALPHA = (2.0 * DEPTH) ** 0.25

OFF_Q = 0
OFF_K = OFF_Q + D_ATTN
OFF_V = OFF_K + D_ATTN
OFF_QI = OFF_V + D_ATTN
OFF_KI = OFF_QI + IDX_HEADS * IDX_DIM
OFF_WI = OFF_KI + IDX_DIM
OFF_CONV = OFF_WI + IDX_HEADS

LANES = 128
PACK = 16
VMEM_LIMIT = 56 * 1024 * 1024

TM = 512
TC = 512
HALO = 32
TQ = 256
TK = 256
TN = 512
TR = 256
TB = 512
EC = 1024
PIECE = 256

LOG2E = 1.4426950408889634
Q_SCALE = HEAD_DIM ** -0.5 * LOG2E
NEG_BIG = -1e30
INT_MIN = -2 ** 31


def _nt_dot(a, b):
    return lax.dot_general(a, b, (((1,), (1,)), ((), ())), preferred_element_type=F32)


def _dot(a, b):
    return jnp.dot(a, b, preferred_element_type=F32)


def _lane_tile(a, n):
    reps = n // LANES
    return a if reps == 1 else jnp.concatenate([a] * reps, axis=1)


def _rope_rows(z, cos, sin):
    x1, x2 = z[:32], z[32:]
    top = x1 * cos[:32] - x2 * sin[:32]
    bot = x2 * cos[32:] + x1 * sin[32:]
    return top, bot


def _proj_kernel(x_ref, wm_ref, ws_ref, wc_ref, cos_ref, sin_ref, kig_ref, kib_ref,
                 qz_ref, qi_ref, v_ref, wi_ref, k_ref, ki_ref, h_ref, kt_scr):
    xb = x_ref[...].astype(BF16)
    cos = cos_ref[...]
    sin = sin_ref[...]
    pT = _nt_dot(wm_ref[...], xb)

    zeros64 = jnp.zeros((HEAD_DIM, TM), BF16)
    for h in range(ATTN_HEADS):
        top, bot = _rope_rows(pT[OFF_Q + h * 64:OFF_Q + (h + 1) * 64], cos, sin)
        qh = (jnp.concatenate([top, bot], axis=0) * Q_SCALE).astype(BF16)
        if h % 2 == 0:
            qz_ref[h * 128:h * 128 + 64, :] = qh
            qz_ref[h * 128 + 64:(h + 1) * 128, :] = zeros64
        else:
            qz_ref[h * 128:h * 128 + 64, :] = zeros64
            qz_ref[h * 128 + 64:(h + 1) * 128, :] = qh
        top, bot = _rope_rows(pT[OFF_K + h * 64:OFF_K + (h + 1) * 64], cos, sin)
        kt_scr[h * 64:h * 64 + 32, :] = top
        kt_scr[h * 64 + 32:(h + 1) * 64, :] = bot
        top, bot = _rope_rows(pT[OFF_QI + h * 64:OFF_QI + (h + 1) * 64], cos, sin)
        qi_ref[h * 64:h * 64 + 32, :] = top.astype(BF16)
        qi_ref[h * 64 + 32:(h + 1) * 64, :] = bot.astype(BF16)
    k_ref[...] = kt_scr[...].T.astype(BF16)

    vT = pT[OFF_V:OFF_V + D_ATTN].astype(BF16)
    for j in range(TM // TK):
        v_ref[j] = vT[:, j * TK:(j + 1) * TK]

    sT = _nt_dot(ws_ref[...], xb)
    z = sT[:IDX_DIM]
    mu = jnp.mean(z, axis=0, keepdims=True)
    var = jnp.mean(jnp.square(z - mu), axis=0, keepdims=True)
    y = (z - mu) * lax.rsqrt(var + LN_EPS) * kig_ref[...] + kib_ref[...]
    top, bot = _rope_rows(y, cos, sin)
    kiT = jnp.concatenate([top, bot, jnp.zeros((LANES - IDX_DIM, TM), F32)], axis=0)
    ki_ref[...] = kiT.T[:, :IDX_DIM].astype(BF16)
    wi_ref[...] = sT[IDX_DIM:IDX_DIM + IDX_HEADS] * (IDX_HEADS ** -0.5)

    a = _dot(xb, wc_ref[...])
    h_ref[...] = a[:, :D_CONV] * jax.nn.sigmoid(a[:, D_CONV:])


def _proj(x, wm, ws, wc, cosT, sinT, kig, kib):
    B, L, _ = x.shape
    grid = (B, L // TM)
    full = lambda shape: pl.BlockSpec(shape, lambda b, i: (0,) * len(shape))
    return pl.pallas_call(
        _proj_kernel,
        grid=grid,
        in_specs=[
            pl.BlockSpec((None, TM, D_MODEL), lambda b, i: (b, i, 0)),
            full(wm.shape), full(ws.shape), full(wc.shape),
            pl.BlockSpec((HEAD_DIM, TM), lambda b, i: (0, i)),
            pl.BlockSpec((HEAD_DIM, TM), lambda b, i: (0, i)),
            full(kig.shape), full(kib.shape),
        ],
        out_specs=[
            pl.BlockSpec((None, ATTN_HEADS * 128, TM), lambda b, i: (b, 0, i)),
            pl.BlockSpec((None, D_ATTN, TM), lambda b, i: (b, 0, i)),
            pl.BlockSpec((None, TM // TK, D_ATTN, TK), lambda b, i: (b, i, 0, 0)),
            pl.BlockSpec((None, IDX_HEADS, TM), lambda b, i: (b, 0, i)),
            pl.BlockSpec((None, TM, D_ATTN), lambda b, i: (b, i, 0)),
            pl.BlockSpec((None, TM, IDX_DIM), lambda b, i: (b, i, 0)),
            pl.BlockSpec((None, TM, D_CONV), lambda b, i: (b, i, 0)),
        ],
        out_shape=[
            jax.ShapeDtypeStruct((B, ATTN_HEADS * 128, L), BF16),
            jax.ShapeDtypeStruct((B, D_ATTN, L), BF16),
            jax.ShapeDtypeStruct((B, L // TK, D_ATTN, TK), BF16),
            jax.ShapeDtypeStruct((B, IDX_HEADS, L), F32),
            jax.ShapeDtypeStruct((B, L, D_ATTN), BF16),
            jax.ShapeDtypeStruct((B, L, IDX_DIM), BF16),
            jax.ShapeDtypeStruct((B, L, D_CONV), F32),
        ],
        scratch_shapes=[pltpu.VMEM((D_ATTN, TM), F32)],
        compiler_params=pltpu.CompilerParams(
            dimension_semantics=("parallel", "parallel"), vmem_limit_bytes=VMEM_LIMIT),
        name="proj",
    )(x, wm, ws, wc, cosT, sinT, kig, kib)


def _conv_kernel(h_ref, halo_ref, w_ref, b_ref, g_ref, beta_ref, o_ref, buf, acc):
    i = pl.program_id(1)
    buf[0:HALO, :] = jnp.where(i > 0, halo_ref[...], 0.0)
    buf[HALO:HALO + TC, :] = h_ref[...]
    base = HALO - (CONV_WIDTH - 1)
    for c in range(D_CONV // LANES):
        cs = slice(c * LANES, (c + 1) * LANES)
        for r in range(TC // LANES):
            a = jnp.zeros((LANES, LANES), F32)
            for j in range(CONV_WIDTH):
                s = base + j + r * LANES
                a = a + buf[s:s + LANES, cs] * w_ref[j:j + 1, cs]
            acc[r * LANES:(r + 1) * LANES, cs] = a
    y = acc[...] + b_ref[...]
    mu = jnp.mean(y, axis=-1, keepdims=True)
    var = jnp.mean(jnp.square(y - mu), axis=-1, keepdims=True)
    y = (y - mu) * lax.rsqrt(var + LN_EPS) * g_ref[...] + beta_ref[...]
    y = y * jax.nn.sigmoid(y)
    o_ref[...] = y.T.astype(BF16)


def _conv(h, w, b, g, beta):
    B, L, _ = h.shape
    full = lambda shape: pl.BlockSpec(shape, lambda b_, i: (0,) * len(shape))
    return pl.pallas_call(
        _conv_kernel,
        grid=(B, L // TC),
        in_specs=[
            pl.BlockSpec((None, TC, D_CONV), lambda b_, i: (b_, i, 0)),
            pl.BlockSpec((None, HALO, D_CONV),
                         lambda b_, i: (b_, jnp.maximum(i * (TC // HALO) - 1, 0), 0)),
            full(w.shape), full(b.shape), full(g.shape), full(beta.shape),
        ],
        out_specs=pl.BlockSpec((None, D_CONV, TC), lambda b_, i: (b_, 0, i)),
        out_shape=jax.ShapeDtypeStruct((B, D_CONV, L), BF16),
        scratch_shapes=[pltpu.VMEM((HALO + TC, D_CONV), F32), pltpu.VMEM((TC, D_CONV), F32)],
        compiler_params=pltpu.CompilerParams(
            dimension_semantics=("parallel", "parallel"), vmem_limit_bytes=VMEM_LIMIT),
        name="conv",
    )(h, h, w, b, g, beta)


def _dsa_kernel(n_sel, qz_ref, qi_ref, wi_ref, k_ref, ki_ref, v_ref, o_ref,
                keys_ref, acc_ref, m_ref, l_ref, bias_ref, att_ref):
    qb = pl.program_id(1)
    nc = qb + 1
    rows = lax.broadcasted_iota(I32, (TK, TQ), 0)
    cols = lax.broadcasted_iota(I32, (TK, TQ), 1)

    def score_chunk(c, carry):
        kic = ki_ref[pl.ds(pl.multiple_of(c * TK, TK), TK), :]
        s = jnp.zeros((TK, TQ), F32)
        for h in range(IDX_HEADS):
            z = _dot(kic, qi_ref[h * IDX_DIM:(h + 1) * IDX_DIM, :])
            s = s + wi_ref[h:h + 1, :] * jnp.maximum(z, 0.0)
        bits = pltpu.bitcast(s, I32)
        key = jnp.where(bits < 0, bits ^ jnp.int32(0x7FFFFFFF), bits)
        valid = (c * TK + rows) <= (qb * TQ + cols)
        keys_ref[c] = jnp.where(valid, key, jnp.int32(INT_MIN))
        return carry

    lax.fori_loop(0, nc, score_chunk, 0)

    @pl.when(nc % 2 == 1)
    def _():
        keys_ref[nc] = jnp.full((TK, TQ), INT_MIN, I32)

    def count_ge(cand):
        def body(i, cnt):
            a = jnp.sum((keys_ref[2 * i] >= cand).astype(I32), axis=0, keepdims=True)
            b = jnp.sum((keys_ref[2 * i + 1] >= cand).astype(I32), axis=0, keepdims=True)
            return cnt + (a + b)
        return lax.fori_loop(0, (nc + 1) // 2, body, jnp.zeros((1, TQ), I32))

    zero = jnp.zeros((1, TQ), I32)
    prefix = jnp.where(count_ge(zero) >= n_sel, zero, jnp.int32(INT_MIN))

    def bit_step(i, prefix):
        cand = prefix + lax.shift_left(jnp.int32(1), 30 - i)
        return jnp.where(count_ge(cand) >= n_sel, cand, prefix)

    prefix = lax.fori_loop(0, 31, bit_step, prefix)
    thr = jnp.maximum(prefix, jnp.int32(INT_MIN + 1))

    acc_ref[...] = jnp.zeros_like(acc_ref)
    m_ref[...] = jnp.full_like(m_ref, NEG_BIG)
    l_ref[...] = jnp.zeros_like(l_ref)

    def attn_chunk(c, carry):
        bias_ref[...] = jnp.where(keys_ref[c] >= thr, 0.0, NEG_BIG)
        start = pl.multiple_of(c * TK, TK)

        def logits(h):
            kc = k_ref[pl.ds(start, TK), (h // 2) * 128:(h // 2 + 1) * 128]
            att_ref[h] = _dot(kc, qz_ref[h * 128:(h + 1) * 128, :]) + bias_ref[...]

        logits(0)
        for h in range(ATTN_HEADS):
            if h + 1 < ATTN_HEADS:
                logits(h + 1)
            m_old = m_ref[h:h + 1, :]
            m_new = jnp.maximum(m_old, jnp.max(att_ref[h], axis=0, keepdims=True))
            p = jnp.exp2(att_ref[h] - m_new)
            alpha = jnp.exp2(m_old - m_new)
            l_ref[h:h + 1, :] = alpha * l_ref[h:h + 1, :] + jnp.sum(p, axis=0, keepdims=True)
            pv = _dot(v_ref[c, h * 64:(h + 1) * 64, :], p.astype(BF16))
            acc_ref[h * 64:(h + 1) * 64, :] = alpha * acc_ref[h * 64:(h + 1) * 64, :] + pv
            m_ref[h:h + 1, :] = m_new
        return carry

    lax.fori_loop(0, nc, attn_chunk, 0)

    for h in range(ATTN_HEADS):
        o_ref[h * 64:(h + 1) * 64, :] = (
            acc_ref[h * 64:(h + 1) * 64, :] / l_ref[h:h + 1, :]).astype(BF16)


def _dsa(qz, qiT, wiT, k, ki, vTc):
    B, _, L = qiT.shape
    n_sel = min(TOPK_MAX, L // 4)
    nck = L // TK
    return pl.pallas_call(
        functools.partial(_dsa_kernel, n_sel),
        grid=(B, L // TQ),
        in_specs=[
            pl.BlockSpec((None, ATTN_HEADS * 128, TQ), lambda b, q: (b, 0, q)),
            pl.BlockSpec((None, D_ATTN, TQ), lambda b, q: (b, 0, q)),
            pl.BlockSpec((None, IDX_HEADS, TQ), lambda b, q: (b, 0, q)),
            pl.BlockSpec((None, L, D_ATTN), lambda b, q: (b, 0, 0)),
            pl.BlockSpec((None, L, IDX_DIM), lambda b, q: (b, 0, 0)),
            pl.BlockSpec((None, nck, D_ATTN, TK), lambda b, q: (b, 0, 0, 0)),
        ],
        out_specs=pl.BlockSpec((None, D_ATTN, TQ), lambda b, q: (b, 0, q)),
        out_shape=jax.ShapeDtypeStruct((B, D_ATTN, L), BF16),
        scratch_shapes=[
            pltpu.VMEM((nck, TK, TQ), I32),
            pltpu.VMEM((D_ATTN, TQ), F32),
            pltpu.VMEM((ATTN_HEADS, TQ), F32),
            pltpu.VMEM((ATTN_HEADS, TQ), F32),
            pltpu.VMEM((TK, TQ), F32),
            pltpu.VMEM((ATTN_HEADS, TK, TQ), F32),
        ],
        compiler_params=pltpu.CompilerParams(
            dimension_semantics=("parallel", "arbitrary"), vmem_limit_bytes=VMEM_LIMIT),
        name="dsa",
    )(qz, qiT, wiT, k, ki, vTc)


def _ln_rows(y, g, b):
    mu = jnp.mean(y, axis=0, keepdims=True)
    var = jnp.mean(jnp.square(y - mu), axis=0, keepdims=True)
    return (y - mu) * lax.rsqrt(var + LN_EPS) * g + b


def _mix_kernel(a_ref, c_ref, x_ref, wa_ref, wc_ref, g_ref, b_ref, o_ref, ob_ref):
    mix = _dot(wa_ref[...], a_ref[...]) + _dot(wc_ref[...], c_ref[...])
    y = DEEPNORM_ALPHA * x_ref[...].T + mix
    x1 = _ln_rows(y, _lane_tile(g_ref[...], TN), _lane_tile(b_ref[...], TN))
    o_ref[...] = x1
    ob_ref[...] = x1.astype(BF16)


def _mix(attnT, convT, x, woa, woc, g, b):
    B, L, _ = x.shape
    nl = L // TN
    full = lambda shape: pl.BlockSpec(shape, lambda b_, i: (0,) * len(shape))
    return pl.pallas_call(
        _mix_kernel,
        grid=(B, nl),
        in_specs=[
            pl.BlockSpec((None, D_ATTN, TN), lambda b_, i: (b_, 0, i)),
            pl.BlockSpec((None, D_CONV, TN), lambda b_, i: (b_, 0, i)),
            pl.BlockSpec((None, TN, D_MODEL), lambda b_, i: (b_, i, 0)),
            full(woa.shape), full(woc.shape), full(g.shape), full(b.shape),
        ],
        out_specs=[
            pl.BlockSpec((D_MODEL, TN), lambda b_, i: (0, b_ * nl + i)),
            pl.BlockSpec((D_MODEL, TN), lambda b_, i: (0, b_ * nl + i)),
        ],
        out_shape=[
            jax.ShapeDtypeStruct((D_MODEL, B * L), F32),
            jax.ShapeDtypeStruct((D_MODEL, B * L), BF16),
        ],
        compiler_params=pltpu.CompilerParams(
            dimension_semantics=("parallel", "parallel"), vmem_limit_bytes=VMEM_LIMIT),
        name="mix_ln1",
    )(attnT, convT, x, woa, woc, g, b)


def _extract_top(s, n_rows, steps, exact_ties):
    T = s.shape[1]
    iota = lax.broadcasted_iota(I32, (n_rows, T), 0).astype(F32)
    rank = jnp.full((n_rows, T), float(steps), F32)
    vals = []
    for a in range(steps):
        cur = jnp.max(s, axis=0, keepdims=True)
        hit = s == cur
        if exact_ties:
            first = jnp.min(jnp.where(hit, iota, float(n_rows)), axis=0, keepdims=True)
            hit = iota == first
        rank = jnp.where(hit, float(a), rank)
        s = jnp.where(hit, -jnp.inf, s)
        vals.append(cur)
    removed = jnp.sum(jnp.where(rank < float(steps), 1.0, 0.0), axis=0, keepdims=True)
    return rank, vals, removed


def _pack_words(x):
    bits = pltpu.bitcast(x.astype(BF16).astype(F32), I32)
    return bits | lax.shift_right_logical(bits, 16)


def _route_heads(qT, s1_ref, s2_ref, r2_ref, c_ref, e1_ref, e2_ref, exact_ties):
    neg_inf = jnp.full((8, TR), -jnp.inf, F32)
    sub8 = lax.broadcasted_iota(I32, (8, TR), 0)
    worst = jnp.zeros((1, TR), F32)
    for h in range(PEER_HEADS):
        q1 = qT[h * 128:h * 128 + PEER_HALF].astype(BF16)
        q2 = qT[h * 128 + PEER_HALF:(h + 1) * 128].astype(BF16)
        s1 = _dot(s1_ref[h], q1)
        s2 = _dot(s2_ref[h], q2)
        r1, v1, n1 = _extract_top(s1, PEER_NKEYS, PEER_TOPK, exact_ties)
        r2, v2, n2 = _extract_top(s2, PEER_NKEYS, PEER_TOPK, exact_ties)
        v1m = jnp.concatenate(v1, axis=0)
        v2m = jnp.concatenate(v2, axis=0)
        blocks = [v1[0] + v2m[:8], v1[0] + v2m[8:]]
        for a in range(1, 8):
            blocks.append(jnp.where(sub8 < PEER_TOPK // (a + 1), v1[a] + v2m[:8], neg_inf))
        blocks.append(v1m[8:] + v2[0])
        cand = jnp.concatenate(blocks, axis=0)
        rc, _, n3 = _extract_top(cand, cand.shape[0], PEER_TOPK, exact_ties)
        worst = jnp.maximum(worst, jnp.maximum(jnp.maximum(n1, n2), n3))
        chosen = rc < float(PEER_TOPK)
        top = v1[0] + v2[0]
        z = jnp.sum(jnp.where(chosen, jnp.exp(cand - top), 0.0), axis=0, keepdims=True)
        chosen_f = chosen.astype(F32)
        per_a = [jnp.sum(chosen_f[:16], axis=0, keepdims=True)]
        for a in range(1, 8):
            per_a.append(jnp.sum(chosen_f[8 + 8 * a:16 + 8 * a], axis=0, keepdims=True))
        for a in range(8, PEER_TOPK):
            per_a.append(chosen_f[64 + a:65 + a])
        cnt = jnp.zeros((PEER_NKEYS, TR), F32)
        for a in range(PEER_TOPK):
            cnt = jnp.where(r1 == float(a), per_a[a], cnt)
        r2_ref[h] = r2.astype(BF16)
        c_ref[h] = _pack_words(cnt)
        e1_ref[h] = _pack_words(jnp.where(r1 < float(PEER_TOPK), jnp.exp(s1 - v1[0]), 0.0) / z)
        e2_ref[h] = jnp.where(r2 < float(PEER_TOPK), jnp.exp(s2 - v2[0]), 0.0).astype(BF16)
    return worst


def _route_kernel(x_ref, wq_ref, s1_ref, s2_ref, r2_ref, c_ref, e1_ref, e2_ref):
    qT = _dot(wq_ref[...], x_ref[...])
    outs = (r2_ref, c_ref, e1_ref, e2_ref)
    worst = _route_heads(qT, s1_ref, s2_ref, *outs, exact_ties=False)

    @pl.when(jnp.max(worst) > float(PEER_TOPK))
    def _():
        _route_heads(qT, s1_ref, s2_ref, *outs, exact_ties=True)


def _route(x1b, wqT, sub1, sub2):
    _, N = x1b.shape
    full = lambda shape: pl.BlockSpec(shape, lambda i: (0,) * len(shape))
    spec = pl.BlockSpec((PEER_HEADS, PEER_NKEYS, TR), lambda i: (0, 0, i))
    i32 = jax.ShapeDtypeStruct((PEER_HEADS, PEER_NKEYS, N), I32)
    b16 = jax.ShapeDtypeStruct((PEER_HEADS, PEER_NKEYS, N), BF16)
    return pl.pallas_call(
        _route_kernel,
        grid=(N // TR,),
        in_specs=[pl.BlockSpec((D_MODEL, TR), lambda i: (0, i)),
                  full(wqT.shape), full(sub1.shape), full(sub2.shape)],
        out_specs=[spec, spec, spec, spec],
        out_shape=[b16, i32, i32, b16],
        compiler_params=pltpu.CompilerParams(
            dimension_semantics=("parallel",), vmem_limit_bytes=VMEM_LIMIT),
        name="peer_route",
    )(x1b, wqT, sub1, sub2)


def _peer_kernel(xb_ref, x_ref, r2_ref, c_ref, e1_ref, e2_ref, u_ref, vt_ref, g_ref, b_ref,
                 o_ref, acc_ref, hh_ref, gh_ref):
    c = pl.program_id(1)

    @pl.when(c == 0)
    def _():
        acc_ref[...] = jnp.zeros_like(acc_ref)

    def packed_rows(words):
        return pltpu.bitcast(jnp.broadcast_to(words, (PEER_NKEYS // 2, TB)), BF16)

    zero = jnp.zeros((), BF16)
    per_piece = PIECE // PEER_NKEYS

    def gates(k):
        for s in range(k * per_piece, (k + 1) * per_piece):
            g = None
            for h in range(PEER_HEADS):
                cb = packed_rows(c_ref[h, s:s + 1, :])
                eb = packed_rows(e1_ref[h, s:s + 1, :])
                t = jnp.where(r2_ref[h] < cb, e2_ref[h], zero) * eb
                g = t if g is None else g + t
            gh_ref[s * PEER_NKEYS:(s + 1) * PEER_NKEYS, :] = g

    def up(k):
        ps = slice(k * PIECE, (k + 1) * PIECE)
        hh_ref[ps, :] = _dot(u_ref[ps, :], xb_ref[...])

    def activate(k):
        ps = slice(k * PIECE, (k + 1) * PIECE)
        hh = hh_ref[ps, :]
        act = (hh * (1.0 + lax.erf(hh * (2.0 ** 0.5)))).astype(BF16)
        gh_ref[ps, :] = gh_ref[ps, :] * act

    def down(k0, k1):
        ps = slice(k0 * PIECE, k1 * PIECE)
        acc_ref[...] += _dot(vt_ref[:, ps], gh_ref[ps, :])

    n_piece = EC // PIECE
    down_every = max(n_piece // 2, 1)
    up(0)
    for k in range(n_piece):
        if k + 1 < n_piece:
            up(k + 1)
        gates(k)
        activate(k)
        if (k + 1) % down_every == 0:
            down(k + 1 - down_every, k + 1)

    @pl.when(c == pl.num_programs(1) - 1)
    def _():
        y = DEEPNORM_ALPHA * x_ref[...] + acc_ref[...]
        x2 = _ln_rows(y, _lane_tile(g_ref[...], TB), _lane_tile(b_ref[...], TB))
        o_ref[...] = x2.T


def _peer(x1b, x1, r2, cnt, e1, e2, u, vt, g, b):
    _, N = x1.shape
    rows = EC // PEER_NKEYS
    full = lambda shape: pl.BlockSpec(shape, lambda j, c: (0,) * len(shape))
    tok = pl.BlockSpec((PEER_HEADS, PEER_NKEYS, TB), lambda j, c: (0, 0, j))
    row = pl.BlockSpec((PEER_HEADS, rows, TB), lambda j, c: (0, c, j))
    return pl.pallas_call(
        _peer_kernel,
        grid=(N // TB, PEER_NEXPERTS // EC),
        in_specs=[
            pl.BlockSpec((D_MODEL, TB), lambda j, c: (0, j)),
            pl.BlockSpec((D_MODEL, TB), lambda j, c: (0, j)),
            tok, row, row, tok,
            pl.BlockSpec((EC, D_MODEL), lambda j, c: (c, 0)),
            pl.BlockSpec((D_MODEL, EC), lambda j, c: (0, c)),
            full(g.shape), full(b.shape),
        ],
        out_specs=pl.BlockSpec((TB, D_MODEL), lambda j, c: (j, 0)),
        out_shape=jax.ShapeDtypeStruct((N, D_MODEL), F32),
        scratch_shapes=[pltpu.VMEM((D_MODEL, TB), F32),
                        pltpu.VMEM((EC, TB), F32),
                        pltpu.VMEM((EC, TB), BF16)],
        compiler_params=pltpu.CompilerParams(
            dimension_semantics=("parallel", "arbitrary"), vmem_limit_bytes=VMEM_LIMIT),
        name="peer_dense",
    )(x1b, x1, r2, cnt, e1, e2, u, vt, g, b)


def _col_bcast(v, n):
    return jnp.broadcast_to(v.astype(F32)[:, None], (v.shape[0], n))


def kernel(x, w_in, idx_k_ln_g, idx_k_ln_b, conv_dw_w, conv_dw_b, conv_ln_g, conv_ln_b,
           w_out, ln1_g, ln1_b, peer_w_q, peer_sub_keys1, peer_sub_keys2, peer_u, peer_v,
           ln2_g, ln2_b):
    B, L, _ = x.shape
    assert L % TM == 0 and L % TQ == 0 and TQ == TK and (B * L) % TB == 0
    l = 0

    w = w_in[l]
    wT = w.T.astype(BF16)
    wm = jnp.concatenate([wT[OFF_Q:OFF_KI]], axis=0)
    ws = jnp.concatenate([wT[OFF_KI:OFF_CONV],
                          jnp.zeros((LANES - IDX_DIM - IDX_HEADS, D_MODEL), BF16)], axis=0)
    wc = w[:, OFF_CONV:].astype(BF16)
    woT = w_out[l].T.astype(BF16)
    woa, woc = woT[:, :D_ATTN], woT[:, D_ATTN:]
    wqT = peer_w_q[l].T.astype(BF16)
    sub1 = peer_sub_keys1[l].astype(BF16)
    sub2 = peer_sub_keys2[l].astype(BF16)
    u = (0.5 * peer_u[l]).astype(BF16)
    vt = peer_v[l].T.astype(BF16)

    pos = jnp.arange(L, dtype=F32)
    inv_freq = ROPE_THETA ** (-jnp.arange(0, HEAD_DIM, 2, dtype=F32) / HEAD_DIM)
    ang = inv_freq[:, None] * pos[None, :]
    cosT = jnp.concatenate([jnp.cos(ang), jnp.cos(ang)], axis=0)
    sinT = jnp.concatenate([jnp.sin(ang), jnp.sin(ang)], axis=0)

    qz, qiT, vTc, wiT, k, ki, h = _proj(
        x, wm, ws, wc, cosT, sinT, _col_bcast(idx_k_ln_g[l], TM), _col_bcast(idx_k_ln_b[l], TM))

    dw = jnp.concatenate([conv_dw_w[l], jnp.zeros((1, D_CONV), F32)], axis=0)
    convT = _conv(h, dw, conv_dw_b[l][None, :], conv_ln_g[l][None, :], conv_ln_b[l][None, :])

    attnT = _dsa(qz, qiT, wiT, k, ki, vTc)

    x1, x1b = _mix(attnT, convT, x, woa, woc,
                   _col_bcast(ln1_g[l], LANES), _col_bcast(ln1_b[l], LANES))

    r2, cnt, e1, e2 = _route(x1b, wqT, sub1, sub2)
    out = _peer(x1b, x1, r2, cnt, e1, e2, u, vt,
                _col_bcast(ln2_g[l], LANES), _col_bcast(ln2_b[l], LANES))
    return out.reshape(B, L, D_MODEL)
```

```python
import functools

import jax
import jax.numpy as jnp
from jax import lax
from jax.experimental import pallas as pl
from jax.experimental.pallas import tpu as pltpu

F32 = jnp.float32
BF16 = jnp.bfloat16
I32 = jnp.int32

D_MODEL = 1024
ATTN_HEADS = 8
HEAD_DIM = 64
D_ATTN = ATTN_HEADS * HEAD_DIM
D_CONV = 512
CONV_WIDTH = 31
IDX_HEADS = 8
IDX_DIM = 64
TOPK_MAX = 256
ROPE_THETA = 10000.0
PEER_HEADS = 8
PEER_NKEYS = 128
PEER_NEXPERTS = PEER_NKEYS * PEER_NKEYS
PEER_HALF = 64
PEER_TOPK = 16
LN_EPS = 1e-5
DEPTH = 1
DEEPNORM_ALPHA = (2.0 * DEPTH) ** 0.25

OFF_Q = 0
OFF_K = OFF_Q + D_ATTN
OFF_V = OFF_K + D_ATTN
OFF_QI = OFF_V + D_ATTN
OFF_KI = OFF_QI + IDX_HEADS * IDX_DIM
OFF_WI = OFF_KI + IDX_DIM
OFF_CONV = OFF_WI + IDX_HEADS

LANES = 128
PACK = 16
VMEM_LIMIT = 56 * 1024 * 1024

TM = 512
TC = 512
HALO = 32
TQ = 256
TK = 256
LOOKAHEAD = 4
TN = 512
TR = 256
TB = 512
EC = 1024
PIECE = 256

LOG2E = 1.4426950408889634
Q_SCALE = HEAD_DIM ** -0.5 * LOG2E
NEG_BIG = -1e30
INT_MIN = -2 ** 31


def _nt_dot(a, b):
    return lax.dot_general(a, b, (((1,), (1,)), ((), ())), preferred_element_type=F32)


def _dot(a, b):
    return jnp.dot(a, b, preferred_element_type=F32)


def _lane_tile(a, n):
    reps = n // LANES
    return a if reps == 1 else jnp.concatenate([a] * reps, axis=1)


def _rope_rows(z, cos, sin):
    x1, x2 = z[:32], z[32:]
    top = x1 * cos[:32] - x2 * sin[:32]
    bot = x2 * cos[32:] + x1 * sin[32:]
    return top, bot


def _proj_kernel(x_ref, wm_ref, ws_ref, wc_ref, cos_ref, sin_ref, kig_ref, kib_ref,
                 qz_ref, qi_ref, v_ref, wi_ref, k_ref, ki_ref, h_ref, kt_scr):
    xb = x_ref[...].astype(BF16)
    cos = cos_ref[...]
    sin = sin_ref[...]
    pT = _nt_dot(wm_ref[...], xb)

    zeros64 = jnp.zeros((HEAD_DIM, TM), BF16)
    for h in range(ATTN_HEADS):
        top, bot = _rope_rows(pT[OFF_Q + h * 64:OFF_Q + (h + 1) * 64], cos, sin)
        qh = (jnp.concatenate([top, bot], axis=0) * Q_SCALE).astype(BF16)
        if h % 2 == 0:
            qz_ref[h * 128:h * 128 + 64, :] = qh
            qz_ref[h * 128 + 64:(h + 1) * 128, :] = zeros64
        else:
            qz_ref[h * 128:h * 128 + 64, :] = zeros64
            qz_ref[h * 128 + 64:(h + 1) * 128, :] = qh
        top, bot = _rope_rows(pT[OFF_K + h * 64:OFF_K + (h + 1) * 64], cos, sin)
        kt_scr[h * 64:h * 64 + 32, :] = top
        kt_scr[h * 64 + 32:(h + 1) * 64, :] = bot
        top, bot = _rope_rows(pT[OFF_QI + h * 64:OFF_QI + (h + 1) * 64], cos, sin)
        qi_ref[h * 64:h * 64 + 32, :] = top.astype(BF16)
        qi_ref[h * 64 + 32:(h + 1) * 64, :] = bot.astype(BF16)
    k_ref[...] = kt_scr[...].T.astype(BF16)

    vT = pT[OFF_V:OFF_V + D_ATTN].astype(BF16)
    for j in range(TM // TK):
        v_ref[j] = vT[:, j * TK:(j + 1) * TK]

    sT = _nt_dot(ws_ref[...], xb)
    z = sT[:IDX_DIM]
    mu = jnp.mean(z, axis=0, keepdims=True)
    var = jnp.mean(jnp.square(z - mu), axis=0, keepdims=True)
    y = (z - mu) * lax.rsqrt(var + LN_EPS) * kig_ref[...] + kib_ref[...]
    top, bot = _rope_rows(y, cos, sin)
    kiT = jnp.concatenate([top, bot, jnp.zeros((LANES - IDX_DIM, TM), F32)], axis=0)
    ki_ref[...] = kiT.T[:, :IDX_DIM].astype(BF16)
    wi_ref[...] = sT[IDX_DIM:IDX_DIM + IDX_HEADS] * (IDX_HEADS ** -0.5)

    a = _dot(xb, wc_ref[...])
    h_ref[...] = a[:, :D_CONV] * jax.nn.sigmoid(a[:, D_CONV:])


def _proj(x, wm, ws, wc, cosT, sinT, kig, kib):
    B, L, _ = x.shape
    grid = (B, L // TM)
    full = lambda shape: pl.BlockSpec(shape, lambda b, i: (0,) * len(shape))
    return pl.pallas_call(
        _proj_kernel,
        grid=grid,
        in_specs=[
            pl.BlockSpec((None, TM, D_MODEL), lambda b, i: (b, i, 0)),
            full(wm.shape), full(ws.shape), full(wc.shape),
            pl.BlockSpec((HEAD_DIM, TM), lambda b, i: (0, i)),
            pl.BlockSpec((HEAD_DIM, TM), lambda b, i: (0, i)),
            full(kig.shape), full(kib.shape),
        ],
        out_specs=[
            pl.BlockSpec((None, ATTN_HEADS * 128, TM), lambda b, i: (b, 0, i)),
            pl.BlockSpec((None, D_ATTN, TM), lambda b, i: (b, 0, i)),
            pl.BlockSpec((None, TM // TK, D_ATTN, TK), lambda b, i: (b, i, 0, 0)),
            pl.BlockSpec((None, IDX_HEADS, TM), lambda b, i: (b, 0, i)),
            pl.BlockSpec((None, TM, D_ATTN), lambda b, i: (b, i, 0)),
            pl.BlockSpec((None, TM, IDX_DIM), lambda b, i: (b, i, 0)),
            pl.BlockSpec((None, TM, D_CONV), lambda b, i: (b, i, 0)),
        ],
        out_shape=[
            jax.ShapeDtypeStruct((B, ATTN_HEADS * 128, L), BF16),
            jax.ShapeDtypeStruct((B, D_ATTN, L), BF16),
            jax.ShapeDtypeStruct((B, L // TK, D_ATTN, TK), BF16),
            jax.ShapeDtypeStruct((B, IDX_HEADS, L), F32),
            jax.ShapeDtypeStruct((B, L, D_ATTN), BF16),
            jax.ShapeDtypeStruct((B, L, IDX_DIM), BF16),
            jax.ShapeDtypeStruct((B, L, D_CONV), F32),
        ],
        scratch_shapes=[pltpu.VMEM((D_ATTN, TM), F32)],
        compiler_params=pltpu.CompilerParams(
            dimension_semantics=("parallel", "parallel"), vmem_limit_bytes=VMEM_LIMIT),
        name="proj",
    )(x, wm, ws, wc, cosT, sinT, kig, kib)


def _conv_kernel(h_ref, halo_ref, w_ref, b_ref, g_ref, beta_ref, o_ref, buf, acc):
    i = pl.program_id(1)
    buf[0:HALO, :] = jnp.where(i > 0, halo_ref[...], 0.0)
    buf[HALO:HALO + TC, :] = h_ref[...]
    base = HALO - (CONV_WIDTH - 1)
    for c in range(D_CONV // LANES):
        cs = slice(c * LANES, (c + 1) * LANES)
        for r in range(TC // LANES):
            a = jnp.zeros((LANES, LANES), F32)
            for j in range(CONV_WIDTH):
                s = base + j + r * LANES
                a = a + buf[s:s + LANES, cs] * w_ref[j:j + 1, cs]
            acc[r * LANES:(r + 1) * LANES, cs] = a
    y = acc[...] + b_ref[...]
    mu = jnp.mean(y, axis=-1, keepdims=True)
    var = jnp.mean(jnp.square(y - mu), axis=-1, keepdims=True)
    y = (y - mu) * lax.rsqrt(var + LN_EPS) * g_ref[...] + beta_ref[...]
    y = y * jax.nn.sigmoid(y)
    o_ref[...] = y.T.astype(BF16)


def _conv(h, w, b, g, beta):
    B, L, _ = h.shape
    full = lambda shape: pl.BlockSpec(shape, lambda b_, i: (0,) * len(shape))
    return pl.pallas_call(
        _conv_kernel,
        grid=(B, L // TC),
        in_specs=[
            pl.BlockSpec((None, TC, D_CONV), lambda b_, i: (b_, i, 0)),
            pl.BlockSpec((None, HALO, D_CONV),
                         lambda b_, i: (b_, jnp.maximum(i * (TC // HALO) - 1, 0), 0)),
            full(w.shape), full(b.shape), full(g.shape), full(beta.shape),
        ],
        out_specs=pl.BlockSpec((None, D_CONV, TC), lambda b_, i: (b_, 0, i)),
        out_shape=jax.ShapeDtypeStruct((B, D_CONV, L), BF16),
        scratch_shapes=[pltpu.VMEM((HALO + TC, D_CONV), F32), pltpu.VMEM((TC, D_CONV), F32)],
        compiler_params=pltpu.CompilerParams(
            dimension_semantics=("parallel", "parallel"), vmem_limit_bytes=VMEM_LIMIT),
        name="conv",
    )(h, h, w, b, g, beta)


def _dsa_kernel(n_sel, qz_ref, qi_ref, wi_ref, k_ref, ki_ref, v_ref, ltri_ref, o_ref,
                keys_ref, acc_ref, m_ref, l_ref, bias_ref, att_ref, ties_ref):
    qb = pl.program_id(1)
    nc = qb + 1
    rows = lax.broadcasted_iota(I32, (TK, TQ), 0)
    cols = lax.broadcasted_iota(I32, (TK, TQ), 1)

    def score_chunk(c, carry):
        kic = ki_ref[pl.ds(pl.multiple_of(c * TK, TK), TK), :]
        s = jnp.zeros((TK, TQ), F32)
        for h in range(IDX_HEADS):
            z = _dot(kic, qi_ref[h * IDX_DIM:(h + 1) * IDX_DIM, :])
            s = s + wi_ref[h:h + 1, :] * jnp.maximum(z, 0.0)
        s = jnp.where(s == 0.0, 0.0, s)
        bits = pltpu.bitcast(s, I32)
        key = jnp.where(bits < 0, bits ^ jnp.int32(0x7FFFFFFF), bits)
        valid = (c * TK + rows) <= (qb * TQ + cols)
        keys_ref[c] = jnp.where(valid, key, jnp.int32(INT_MIN))
        return carry

    lax.fori_loop(0, nc, score_chunk, 0)

    @pl.when(nc % 2 == 1)
    def _():
        keys_ref[nc] = jnp.full((TK, TQ), INT_MIN, I32)

    def count_ge(cand):
        def body(i, cnt):
            a = jnp.sum((keys_ref[2 * i] >= cand).astype(I32), axis=0, keepdims=True)
            b = jnp.sum((keys_ref[2 * i + 1] >= cand).astype(I32), axis=0, keepdims=True)
            return cnt + (a + b)
        return lax.fori_loop(0, (nc + 1) // 2, body, jnp.zeros((1, TQ), I32))

    zero = jnp.zeros((1, TQ), I32)
    prefix = jnp.where(count_ge(zero) >= n_sel, zero, jnp.int32(INT_MIN))

    def bit_step(i, prefix):
        cand = prefix + lax.shift_left(jnp.int32(1), 30 - i)
        return jnp.where(count_ge(cand) >= n_sel, cand, prefix)

    prefix = lax.fori_loop(0, 31, bit_step, prefix)
    thr = jnp.maximum(prefix, jnp.int32(INT_MIN + 1))
    take = (n_sel - count_ge(thr + 1)).astype(F32)

    acc_ref[...] = jnp.zeros_like(acc_ref)
    m_ref[...] = jnp.full_like(m_ref, NEG_BIG)
    l_ref[...] = jnp.zeros_like(l_ref)
    ties_ref[...] = jnp.zeros_like(ties_ref)

    def attn_chunk(c, carry):
        kk = keys_ref[c]
        tie = kk == thr
        tie_b = jnp.where(tie, 1.0, 0.0).astype(BF16)
        before = _dot(ltri_ref[...], tie_b) + ties_ref[...]
        tie_bias = jnp.where(before < take, 0.0, NEG_BIG)
        bias_ref[...] = jnp.where(kk > thr, 0.0, jnp.where(tie, tie_bias, NEG_BIG))
        ties_ref[...] += jnp.sum(tie_b.astype(F32), axis=0, keepdims=True)
        start = pl.multiple_of(c * TK, TK)

        def logits(h):
            kc = k_ref[pl.ds(start, TK), (h // 2) * 128:(h // 2 + 1) * 128]
            att_ref[h] = _dot(kc, qz_ref[h * 128:(h + 1) * 128, :]) + bias_ref[...]

        for h in range(min(LOOKAHEAD, ATTN_HEADS)):
            logits(h)
        for h in range(ATTN_HEADS):
            if h % 2 == 0:
                for hh in range(h + LOOKAHEAD, min(h + LOOKAHEAD + 2, ATTN_HEADS)):
                    logits(hh)
            m_old = m_ref[h:h + 1, :]
            m_new = jnp.maximum(m_old, jnp.max(att_ref[h], axis=0, keepdims=True))
            p = jnp.exp2(att_ref[h] - m_new)
            alpha = jnp.exp2(m_old - m_new)
            l_ref[h:h + 1, :] = alpha * l_ref[h:h + 1, :] + jnp.sum(p, axis=0, keepdims=True)
            pv = _dot(v_ref[c, h * 64:(h + 1) * 64, :], p.astype(BF16))
            acc_ref[h * 64:(h + 1) * 64, :] = alpha * acc_ref[h * 64:(h + 1) * 64, :] + pv
            m_ref[h:h + 1, :] = m_new
        return carry

    lax.fori_loop(0, nc, attn_chunk, 0)

    for h in range(ATTN_HEADS):
        o_ref[h * 64:(h + 1) * 64, :] = (
            acc_ref[h * 64:(h + 1) * 64, :] / l_ref[h:h + 1, :]).astype(BF16)


def _dsa(qz, qiT, wiT, k, ki, vTc):
    B, _, L = qiT.shape
    n_sel = min(TOPK_MAX, L // 4)
    nck = L // TK
    ltri = jnp.tril(jnp.ones((TK, TK), BF16), -1)
    return pl.pallas_call(
        functools.partial(_dsa_kernel, n_sel),
        grid=(B, L // TQ),
        in_specs=[
            pl.BlockSpec((None, ATTN_HEADS * 128, TQ), lambda b, q: (b, 0, q)),
            pl.BlockSpec((None, D_ATTN, TQ), lambda b, q: (b, 0, q)),
            pl.BlockSpec((None, IDX_HEADS, TQ), lambda b, q: (b, 0, q)),
            pl.BlockSpec((None, L, D_ATTN), lambda b, q: (b, 0, 0)),
            pl.BlockSpec((None, L, IDX_DIM), lambda b, q: (b, 0, 0)),
            pl.BlockSpec((None, nck, D_ATTN, TK), lambda b, q: (b, 0, 0, 0)),
            pl.BlockSpec((TK, TK), lambda b, q: (0, 0)),
        ],
        out_specs=pl.BlockSpec((None, D_ATTN, TQ), lambda b, q: (b, 0, q)),
        out_shape=jax.ShapeDtypeStruct((B, D_ATTN, L), BF16),
        scratch_shapes=[
            pltpu.VMEM((nck, TK, TQ), I32),
            pltpu.VMEM((D_ATTN, TQ), F32),
            pltpu.VMEM((ATTN_HEADS, TQ), F32),
            pltpu.VMEM((ATTN_HEADS, TQ), F32),
            pltpu.VMEM((TK, TQ), F32),
            pltpu.VMEM((ATTN_HEADS, TK, TQ), F32),
            pltpu.VMEM((1, TQ), F32),
        ],
        compiler_params=pltpu.CompilerParams(
            dimension_semantics=("parallel", "arbitrary"), vmem_limit_bytes=VMEM_LIMIT),
        name="dsa",
    )(qz, qiT, wiT, k, ki, vTc, ltri)


def _ln_rows(y, g, b):
    mu = jnp.mean(y, axis=0, keepdims=True)
    var = jnp.mean(jnp.square(y - mu), axis=0, keepdims=True)
    return (y - mu) * lax.rsqrt(var + LN_EPS) * g + b


def _mix_kernel(a_ref, c_ref, x_ref, wa_ref, wc_ref, g_ref, b_ref, o_ref, ob_ref):
    mix = _dot(wa_ref[...], a_ref[...]) + _dot(wc_ref[...], c_ref[...])
    y = DEEPNORM_ALPHA * x_ref[...].T + mix
    x1 = _ln_rows(y, _lane_tile(g_ref[...], TN), _lane_tile(b_ref[...], TN))
    o_ref[...] = x1
    ob_ref[...] = x1.astype(BF16)


def _mix(attnT, convT, x, woa, woc, g, b):
    B, L, _ = x.shape
    nl = L // TN
    full = lambda shape: pl.BlockSpec(shape, lambda b_, i: (0,) * len(shape))
    return pl.pallas_call(
        _mix_kernel,
        grid=(B, nl),
        in_specs=[
            pl.BlockSpec((None, D_ATTN, TN), lambda b_, i: (b_, 0, i)),
            pl.BlockSpec((None, D_CONV, TN), lambda b_, i: (b_, 0, i)),
            pl.BlockSpec((None, TN, D_MODEL), lambda b_, i: (b_, i, 0)),
            full(woa.shape), full(woc.shape), full(g.shape), full(b.shape),
        ],
        out_specs=[
            pl.BlockSpec((D_MODEL, TN), lambda b_, i: (0, b_ * nl + i)),
            pl.BlockSpec((D_MODEL, TN), lambda b_, i: (0, b_ * nl + i)),
        ],
        out_shape=[
            jax.ShapeDtypeStruct((D_MODEL, B * L), F32),
            jax.ShapeDtypeStruct((D_MODEL, B * L), BF16),
        ],
        compiler_params=pltpu.CompilerParams(
            dimension_semantics=("parallel", "parallel"), vmem_limit_bytes=VMEM_LIMIT),
        name="mix_ln1",
    )(attnT, convT, x, woa, woc, g, b)


def _extract_top(s, n_rows, steps, exact_ties):
    T = s.shape[1]
    iota = lax.broadcasted_iota(I32, (n_rows, T), 0).astype(F32)
    rank = jnp.full((n_rows, T), float(steps), F32)
    vals = []
    for a in range(steps):
        cur = jnp.max(s, axis=0, keepdims=True)
        hit = s == cur
        if exact_ties:
            first = jnp.min(jnp.where(hit, iota, float(n_rows)), axis=0, keepdims=True)
            hit = iota == first
        rank = jnp.where(hit, float(a), rank)
        s = jnp.where(hit, -jnp.inf, s)
        vals.append(cur)
    removed = jnp.sum(jnp.where(rank < float(steps), 1.0, 0.0), axis=0, keepdims=True)
    return rank, vals, removed


def _pack_words(x):
    bits = pltpu.bitcast(x.astype(BF16).astype(F32), I32)
    return bits | lax.shift_right_logical(bits, 16)


def _route_head(h, qT, s1_ref, s2_ref, r2_ref, c_ref, e1_ref, e2_ref, exact_ties):
    neg_inf = jnp.full((8, TR), -jnp.inf, F32)
    sub8 = lax.broadcasted_iota(I32, (8, TR), 0)
    q1 = qT[h * 128:h * 128 + PEER_HALF].astype(BF16)
    q2 = qT[h * 128 + PEER_HALF:(h + 1) * 128].astype(BF16)
    s1 = _dot(s1_ref[h], q1)
    s2 = _dot(s2_ref[h], q2)
    r1, v1, n1 = _extract_top(s1, PEER_NKEYS, PEER_TOPK, exact_ties)
    r2, v2, n2 = _extract_top(s2, PEER_NKEYS, PEER_TOPK, exact_ties)
    v1m = jnp.concatenate(v1, axis=0)
    v2m = jnp.concatenate(v2, axis=0)
    blocks = [v1[0] + v2m[:8], v1[0] + v2m[8:]]
    for a in range(1, 8):
        blocks.append(jnp.where(sub8 < PEER_TOPK // (a + 1), v1[a] + v2m[:8], neg_inf))
    blocks.append(v1m[8:] + v2[0])
    cand = jnp.concatenate(blocks, axis=0)
    rc, _, n3 = _extract_top(cand, cand.shape[0], PEER_TOPK, exact_ties)
    chosen = rc < float(PEER_TOPK)
    top = v1[0] + v2[0]
    z = jnp.sum(jnp.where(chosen, jnp.exp(cand - top), 0.0), axis=0, keepdims=True)
    chosen_f = chosen.astype(F32)
    per_a = [jnp.sum(chosen_f[:16], axis=0, keepdims=True)]
    for a in range(1, 8):
        per_a.append(jnp.sum(chosen_f[8 + 8 * a:16 + 8 * a], axis=0, keepdims=True))
    for a in range(8, PEER_TOPK):
        per_a.append(chosen_f[64 + a:65 + a])
    cnt = jnp.zeros((PEER_NKEYS, TR), F32)
    for a in range(PEER_TOPK):
        cnt = jnp.where(r1 == float(a), per_a[a], cnt)
    r2_ref[h] = r2.astype(BF16)
    c_ref[h] = _pack_words(cnt)
    e1_ref[h] = _pack_words(jnp.where(r1 < float(PEER_TOPK), jnp.exp(s1 - v1[0]), 0.0) / z)
    e2_ref[h] = jnp.where(r2 < float(PEER_TOPK), jnp.exp(s2 - v2[0]), 0.0).astype(BF16)
    return jnp.maximum(jnp.maximum(n1, n2), n3)


def _route_kernel(x_ref, wq_ref, s1_ref, s2_ref, r2_ref, c_ref, e1_ref, e2_ref, q_scr):
    q_scr[...] = _dot(wq_ref[...], x_ref[...])
    refs = (s1_ref, s2_ref, r2_ref, c_ref, e1_ref, e2_ref)
    worst = [_route_head(h, q_scr, *refs, exact_ties=False) for h in range(PEER_HEADS)]
    for h in range(PEER_HEADS):
        @pl.when(jnp.max(worst[h]) > float(PEER_TOPK))
        def _():
            _route_head(h, q_scr, *refs, exact_ties=True)


def _route(x1b, wqT, sub1, sub2):
    _, N = x1b.shape
    full = lambda shape: pl.BlockSpec(shape, lambda i: (0,) * len(shape))
    spec = pl.BlockSpec((PEER_HEADS, PEER_NKEYS, TR), lambda i: (0, 0, i))
    i32 = jax.ShapeDtypeStruct((PEER_HEADS, PEER_NKEYS, N), I32)
    b16 = jax.ShapeDtypeStruct((PEER_HEADS, PEER_NKEYS, N), BF16)
    return pl.pallas_call(
        _route_kernel,
        grid=(N // TR,),
        in_specs=[pl.BlockSpec((D_MODEL, TR), lambda i: (0, i)),
                  full(wqT.shape), full(sub1.shape), full(sub2.shape)],
        out_specs=[spec, spec, spec, spec],
        out_shape=[b16, i32, i32, b16],
        scratch_shapes=[pltpu.VMEM((D_MODEL, TR), F32)],
        compiler_params=pltpu.CompilerParams(
            dimension_semantics=("parallel",), vmem_limit_bytes=VMEM_LIMIT),
        name="peer_route",
    )(x1b, wqT, sub1, sub2)


def _peer_kernel(xb_ref, x_ref, r2_ref, c_ref, e1_ref, e2_ref, u_ref, vt_ref, g_ref, b_ref,
                 o_ref, acc_ref, hh_ref, gh_ref):
    c = pl.program_id(1)

    @pl.when(c == 0)
    def _():
        acc_ref[...] = jnp.zeros_like(acc_ref)

    def packed_rows(words):
        return pltpu.bitcast(jnp.broadcast_to(words, (PEER_NKEYS // 2, TB)), BF16)

    zero = jnp.zeros((), BF16)
    per_piece = PIECE // PEER_NKEYS

    def gates(k):
        for s in range(k * per_piece, (k + 1) * per_piece):
            g = None
            for h in range(PEER_HEADS):
                cb = packed_rows(c_ref[h, s:s + 1, :])
                eb = packed_rows(e1_ref[h, s:s + 1, :])
                t = jnp.where(r2_ref[h] < cb, e2_ref[h], zero) * eb
                g = t if g is None else g + t
            gh_ref[s * PEER_NKEYS:(s + 1) * PEER_NKEYS, :] = g

    def up(k):
        ps = slice(k * PIECE, (k + 1) * PIECE)
        hh_ref[ps, :] = _dot(u_ref[ps, :], xb_ref[...])

    def activate(k):
        ps = slice(k * PIECE, (k + 1) * PIECE)
        hh = hh_ref[ps, :]
        act = (hh * (1.0 + lax.erf(hh * (2.0 ** 0.5)))).astype(BF16)
        gh_ref[ps, :] = gh_ref[ps, :] * act

    def down(k0, k1):
        ps = slice(k0 * PIECE, k1 * PIECE)
        acc_ref[...] += _dot(vt_ref[:, ps], gh_ref[ps, :])

    n_piece = EC // PIECE
    down_every = max(n_piece // 2, 1)
    up(0)
    for k in range(n_piece):
        if k + 1 < n_piece:
            up(k + 1)
        gates(k)
        activate(k)
        if (k + 1) % down_every == 0:
            down(k + 1 - down_every, k + 1)

    @pl.when(c == pl.num_programs(1) - 1)
    def _():
        y = DEEPNORM_ALPHA * x_ref[...] + acc_ref[...]
        x2 = _ln_rows(y, _lane_tile(g_ref[...], TB), _lane_tile(b_ref[...], TB))
        o_ref[...] = x2.T


def _peer(x1b, x1, r2, cnt, e1, e2, u, vt, g, b):
    _, N = x1.shape
    rows = EC // PEER_NKEYS
    full = lambda shape: pl.BlockSpec(shape, lambda j, c: (0,) * len(shape))
    tok = pl.BlockSpec((PEER_HEADS, PEER_NKEYS, TB), lambda j, c: (0, 0, j))
    row = pl.BlockSpec((PEER_HEADS, rows, TB), lambda j, c: (0, c, j))
    return pl.pallas_call(
        _peer_kernel,
        grid=(N // TB, PEER_NEXPERTS // EC),
        in_specs=[
            pl.BlockSpec((D_MODEL, TB), lambda j, c: (0, j)),
            pl.BlockSpec((D_MODEL, TB), lambda j, c: (0, j)),
            tok, row, row, tok,
            pl.BlockSpec((EC, D_MODEL), lambda j, c: (c, 0)),
            pl.BlockSpec((D_MODEL, EC), lambda j, c: (0, c)),
            full(g.shape), full(b.shape),
        ],
        out_specs=pl.BlockSpec((TB, D_MODEL), lambda j, c: (j, 0)),
        out_shape=jax.ShapeDtypeStruct((N, D_MODEL), F32),
        scratch_shapes=[pltpu.VMEM((D_MODEL, TB), F32),
                        pltpu.VMEM((EC, TB), F32),
                        pltpu.VMEM((EC, TB), BF16)],
        compiler_params=pltpu.CompilerParams(
            dimension_semantics=("parallel", "arbitrary"), vmem_limit_bytes=VMEM_LIMIT),
        name="peer_dense",
    )(x1b, x1, r2, cnt, e1, e2, u, vt, g, b)


def _col_bcast(v, n):
    return jnp.broadcast_to(v.astype(F32)[:, None], (v.shape[0], n))


def kernel(x, w_in, idx_k_ln_g, idx_k_ln_b, conv_dw_w, conv_dw_b, conv_ln_g, conv_ln_b,
           w_out, ln1_g, ln1_b, peer_w_q, peer_sub_keys1, peer_sub_keys2, peer_u, peer_v,
           ln2_g, ln2_b):
    B, L, _ = x.shape
    assert L % TM == 0 and L % TQ == 0 and TQ == TK and (B * L) % TB == 0
    l = 0

    w = w_in[l]
    wT = w.T.astype(BF16)
    wm = jnp.concatenate([wT[OFF_Q:OFF_KI]], axis=0)
    ws = jnp.concatenate([wT[OFF_KI:OFF_CONV],
                          jnp.zeros((LANES - IDX_DIM - IDX_HEADS, D_MODEL), BF16)], axis=0)
    wc = w[:, OFF_CONV:].astype(BF16)
    woT = w_out[l].T.astype(BF16)
    woa, woc = woT[:, :D_ATTN], woT[:, D_ATTN:]
    wqT = peer_w_q[l].T.astype(BF16)
    sub1 = peer_sub_keys1[l].astype(BF16)
    sub2 = peer_sub_keys2[l].astype(BF16)
    u = (0.5 * peer_u[l]).astype(BF16)
    vt = peer_v[l].T.astype(BF16)

    pos = jnp.arange(L, dtype=F32)
    inv_freq = ROPE_THETA ** (-jnp.arange(0, HEAD_DIM, 2, dtype=F32) / HEAD_DIM)
    ang = inv_freq[:, None] * pos[None, :]
    cosT = jnp.concatenate([jnp.cos(ang), jnp.cos(ang)], axis=0)
    sinT = jnp.concatenate([jnp.sin(ang), jnp.sin(ang)], axis=0)

    qz, qiT, vTc, wiT, k, ki, h = _proj(
        x, wm, ws, wc, cosT, sinT, _col_bcast(idx_k_ln_g[l], TM), _col_bcast(idx_k_ln_b[l], TM))

    dw = jnp.concatenate([conv_dw_w[l], jnp.zeros((1, D_CONV), F32)], axis=0)
    convT = _conv(h, dw, conv_dw_b[l][None, :], conv_ln_g[l][None, :], conv_ln_b[l][None, :])

    attnT = _dsa(qz, qiT, wiT, k, ki, vTc)

    x1, x1b = _mix(attnT, convT, x, woa, woc,
                   _col_bcast(ln1_g[l], LANES), _col_bcast(ln1_b[l], LANES))

    r2, cnt, e1, e2 = _route(x1b, wqT, sub1, sub2)
    out = _peer(x1b, x1, r2, cnt, e1, e2, u, vt,
                _col_bcast(ln2_g[l], LANES), _col_bcast(ln2_b[l], LANES))
    return out.reshape(B, L, D_MODEL)
```

```python
import functools

import jax
import jax.numpy as jnp
from jax import lax
from jax.experimental import pallas as pl
from jax.experimental.pallas import tpu as pltpu

F32 = jnp.float32
BF16 = jnp.bfloat16
I32 = jnp.int32

D_MODEL = 1024
ATTN_HEADS = 8
HEAD_DIM = 64
D_ATTN = ATTN_HEADS * HEAD_DIM
D_CONV = 512
CONV_WIDTH = 31
IDX_HEADS = 8
IDX_DIM = 64
TOPK_MAX = 256
ROPE_THETA = 10000.0
PEER_HEADS = 8
PEER_NKEYS = 128
PEER_NEXPERTS = PEER_NKEYS * PEER_NKEYS
PEER_HALF = 64
PEER_TOPK = 16
LN_EPS = 1e-5
DEPTH = 1
DEEPNORM_ALPHA = (2.0 * DEPTH) ** 0.25

OFF_Q = 0
OFF_K = OFF_Q + D_ATTN
OFF_V = OFF_K + D_ATTN
OFF_QI = OFF_V + D_ATTN
OFF_KI = OFF_QI + IDX_HEADS * IDX_DIM
OFF_WI = OFF_KI + IDX_DIM
OFF_CONV = OFF_WI + IDX_HEADS

LANES = 128
SUBLANES = 8
PACK = 16
VMEM_LIMIT = 56 * 1024 * 1024

TM = 512
TC = 512
HALO = 32
TQ = 256
TK = 256
LOOKAHEAD = 4
TN = 512
TR = 256
TB = 512
EC = 1024
PIECE = 256
GROWS = 128

LOG2E = 1.4426950408889634
Q_SCALE = HEAD_DIM ** -0.5 * LOG2E
NEG_BIG = -1e30
INT_MIN = -2 ** 31


def _nt_dot(a, b):
    return lax.dot_general(a, b, (((1,), (1,)), ((), ())), preferred_element_type=F32)


def _dot(a, b):
    return jnp.dot(a, b, preferred_element_type=F32)


def _lane_tile(a, n):
    reps = n // LANES
    return a if reps == 1 else jnp.concatenate([a] * reps, axis=1)


def _rope_rows(z, cos, sin):
    x1, x2 = z[:32], z[32:]
    top = x1 * cos[:32] - x2 * sin[:32]
    bot = x2 * cos[32:] + x1 * sin[32:]
    return top, bot


def _proj_kernel(x_ref, wm_ref, ws_ref, wc_ref, cos_ref, sin_ref, kig_ref, kib_ref,
                 qz_ref, qi_ref, v_ref, wi_ref, k_ref, ki_ref, h_ref, kt_scr):
    xb = x_ref[...].astype(BF16)
    cos = cos_ref[...]
    sin = sin_ref[...]
    pT = _nt_dot(wm_ref[...], xb)

    zeros64 = jnp.zeros((HEAD_DIM, TM), BF16)
    for h in range(ATTN_HEADS):
        top, bot = _rope_rows(pT[OFF_Q + h * 64:OFF_Q + (h + 1) * 64], cos, sin)
        qh = (jnp.concatenate([top, bot], axis=0) * Q_SCALE).astype(BF16)
        if h % 2 == 0:
            qz_ref[h * 128:h * 128 + 64, :] = qh
            qz_ref[h * 128 + 64:(h + 1) * 128, :] = zeros64
        else:
            qz_ref[h * 128:h * 128 + 64, :] = zeros64
            qz_ref[h * 128 + 64:(h + 1) * 128, :] = qh
        top, bot = _rope_rows(pT[OFF_K + h * 64:OFF_K + (h + 1) * 64], cos, sin)
        kt_scr[h * 64:h * 64 + 32, :] = top
        kt_scr[h * 64 + 32:(h + 1) * 64, :] = bot
        top, bot = _rope_rows(pT[OFF_QI + h * 64:OFF_QI + (h + 1) * 64], cos, sin)
        qi_ref[h * 64:h * 64 + 32, :] = top.astype(BF16)
        qi_ref[h * 64 + 32:(h + 1) * 64, :] = bot.astype(BF16)
    k_ref[...] = kt_scr[...].T.astype(BF16)

    vT = pT[OFF_V:OFF_V + D_ATTN].astype(BF16)
    for j in range(TM // TK):
        v_ref[j] = vT[:, j * TK:(j + 1) * TK]

    sT = _nt_dot(ws_ref[...], xb)
    z = sT[:IDX_DIM]
    mu = jnp.mean(z, axis=0, keepdims=True)
    var = jnp.mean(jnp.square(z - mu), axis=0, keepdims=True)
    y = (z - mu) * lax.rsqrt(var + LN_EPS) * kig_ref[...] + kib_ref[...]
    top, bot = _rope_rows(y, cos, sin)
    kiT = jnp.concatenate([top, bot, jnp.zeros((LANES - IDX_DIM, TM), F32)], axis=0)
    ki_ref[...] = kiT.T[:, :IDX_DIM].astype(BF16)
    wi_ref[...] = sT[IDX_DIM:IDX_DIM + IDX_HEADS] * (IDX_HEADS ** -0.5)

    a = _dot(xb, wc_ref[...])
    h_ref[...] = a[:, :D_CONV] * jax.nn.sigmoid(a[:, D_CONV:])


def _proj(x, wm, ws, wc, cosT, sinT, kig, kib):
    B, L, _ = x.shape
    grid = (B, L // TM)
    full = lambda shape: pl.BlockSpec(shape, lambda b, i: (0,) * len(shape))
    return pl.pallas_call(
        _proj_kernel,
        grid=grid,
        in_specs=[
            pl.BlockSpec((None, TM, D_MODEL), lambda b, i: (b, i, 0)),
            full(wm.shape), full(ws.shape), full(wc.shape),
            pl.BlockSpec((HEAD_DIM, TM), lambda b, i: (0, i)),
            pl.BlockSpec((HEAD_DIM, TM), lambda b, i: (0, i)),
            full(kig.shape), full(kib.shape),
        ],
        out_specs=[
            pl.BlockSpec((None, ATTN_HEADS * 128, TM), lambda b, i: (b, 0, i)),
            pl.BlockSpec((None, D_ATTN, TM), lambda b, i: (b, 0, i)),
            pl.BlockSpec((None, TM // TK, D_ATTN, TK), lambda b, i: (b, i, 0, 0)),
            pl.BlockSpec((None, IDX_HEADS, TM), lambda b, i: (b, 0, i)),
            pl.BlockSpec((None, TM, D_ATTN), lambda b, i: (b, i, 0)),
            pl.BlockSpec((None, TM, IDX_DIM), lambda b, i: (b, i, 0)),
            pl.BlockSpec((None, TM, D_CONV), lambda b, i: (b, i, 0)),
        ],
        out_shape=[
            jax.ShapeDtypeStruct((B, ATTN_HEADS * 128, L), BF16),
            jax.ShapeDtypeStruct((B, D_ATTN, L), BF16),
            jax.ShapeDtypeStruct((B, L // TK, D_ATTN, TK), BF16),
            jax.ShapeDtypeStruct((B, IDX_HEADS, L), F32),
            jax.ShapeDtypeStruct((B, L, D_ATTN), BF16),
            jax.ShapeDtypeStruct((B, L, IDX_DIM), BF16),
            jax.ShapeDtypeStruct((B, L, D_CONV), F32),
        ],
        scratch_shapes=[pltpu.VMEM((D_ATTN, TM), F32)],
        compiler_params=pltpu.CompilerParams(
            dimension_semantics=("parallel", "parallel"), vmem_limit_bytes=VMEM_LIMIT),
        name="proj",
    )(x, wm, ws, wc, cosT, sinT, kig, kib)


def _conv_kernel(h_ref, halo_ref, w_ref, b_ref, g_ref, beta_ref, o_ref, buf, shifted, acc):
    i = pl.program_id(1)
    rows = HALO + TC
    buf[0:HALO, :] = jnp.where(i > 0, halo_ref[...], 0.0)
    buf[HALO:rows, :] = h_ref[...]
    buf[rows:rows + SUBLANES, :] = jnp.zeros((SUBLANES, D_CONV), F32)
    for r in range(SUBLANES):
        shifted[r] = buf[r:r + rows, :]
    base = HALO - (CONV_WIDTH - 1)
    for c in range(D_CONV // LANES):
        cs = slice(c * LANES, (c + 1) * LANES)
        for rb in range(TC // LANES):
            a = jnp.zeros((LANES, LANES), F32)
            for j in range(CONV_WIDTH):
                phase = (base + j) % SUBLANES
                s = base + j - phase + rb * LANES
                a = a + shifted[phase, s:s + LANES, cs] * w_ref[j:j + 1, cs]
            acc[rb * LANES:(rb + 1) * LANES, cs] = a
    y = acc[...] + b_ref[...]
    mu = jnp.mean(y, axis=-1, keepdims=True)
    var = jnp.mean(jnp.square(y - mu), axis=-1, keepdims=True)
    y = (y - mu) * lax.rsqrt(var + LN_EPS) * g_ref[...] + beta_ref[...]
    y = y * jax.nn.sigmoid(y)
    o_ref[...] = y.T.astype(BF16)


def _conv(h, w, b, g, beta):
    B, L, _ = h.shape
    full = lambda shape: pl.BlockSpec(shape, lambda b_, i: (0,) * len(shape))
    return pl.pallas_call(
        _conv_kernel,
        grid=(B, L // TC),
        in_specs=[
            pl.BlockSpec((None, TC, D_CONV), lambda b_, i: (b_, i, 0)),
            pl.BlockSpec((None, HALO, D_CONV),
                         lambda b_, i: (b_, jnp.maximum(i * (TC // HALO) - 1, 0), 0)),
            full(w.shape), full(b.shape), full(g.shape), full(beta.shape),
        ],
        out_specs=pl.BlockSpec((None, D_CONV, TC), lambda b_, i: (b_, 0, i)),
        out_shape=jax.ShapeDtypeStruct((B, D_CONV, L), BF16),
        scratch_shapes=[pltpu.VMEM((HALO + TC + SUBLANES, D_CONV), F32),
                        pltpu.VMEM((SUBLANES, HALO + TC, D_CONV), F32),
                        pltpu.VMEM((TC, D_CONV), F32)],
        compiler_params=pltpu.CompilerParams(
            dimension_semantics=("parallel", "parallel"), vmem_limit_bytes=VMEM_LIMIT),
        name="conv",
    )(h, h, w, b, g, beta)


def _dsa_kernel(n_sel, qz_ref, qi_ref, wi_ref, k_ref, ki_ref, v_ref, ltri_ref, o_ref,
                keys_ref, acc_ref, m_ref, l_ref, bias_ref, att_ref, ties_ref):
    qb = pl.program_id(1)
    nc = qb + 1
    rows = lax.broadcasted_iota(I32, (TK, TQ), 0)
    cols = lax.broadcasted_iota(I32, (TK, TQ), 1)

    def score_chunk(c, carry):
        kic = ki_ref[pl.ds(pl.multiple_of(c * TK, TK), TK), :]
        s = jnp.zeros((TK, TQ), F32)
        for h in range(IDX_HEADS):
            z = _dot(kic, qi_ref[h * IDX_DIM:(h + 1) * IDX_DIM, :])
            s = s + wi_ref[h:h + 1, :] * jnp.maximum(z, 0.0)
        s = jnp.where(s == 0.0, 0.0, s)
        bits = pltpu.bitcast(s, I32)
        key = jnp.where(bits < 0, bits ^ jnp.int32(0x7FFFFFFF), bits)
        valid = (c * TK + rows) <= (qb * TQ + cols)
        keys_ref[c] = jnp.where(valid, key, jnp.int32(INT_MIN))
        return carry

    lax.fori_loop(0, nc, score_chunk, 0)

    @pl.when(nc % 2 == 1)
    def _():
        keys_ref[nc] = jnp.full((TK, TQ), INT_MIN, I32)

    def count_ge(cand):
        def body(i, cnt):
            a = jnp.sum((keys_ref[2 * i] >= cand).astype(I32), axis=0, keepdims=True)
            b = jnp.sum((keys_ref[2 * i + 1] >= cand).astype(I32), axis=0, keepdims=True)
            return cnt + (a + b)
        return lax.fori_loop(0, (nc + 1) // 2, body, jnp.zeros((1, TQ), I32))

    zero = jnp.zeros((1, TQ), I32)
    prefix = jnp.where(count_ge(zero) >= n_sel, zero, jnp.int32(INT_MIN))

    def bit_step(i, prefix):
        cand = prefix + lax.shift_left(jnp.int32(1), 30 - i)
        return jnp.where(count_ge(cand) >= n_sel, cand, prefix)

    prefix = lax.fori_loop(0, 31, bit_step, prefix)
    thr = jnp.maximum(prefix, jnp.int32(INT_MIN + 1))
    take = (n_sel - count_ge(thr + 1)).astype(F32)

    acc_ref[...] = jnp.zeros_like(acc_ref)
    m_ref[...] = jnp.full_like(m_ref, NEG_BIG)
    l_ref[...] = jnp.zeros_like(l_ref)
    ties_ref[...] = jnp.zeros_like(ties_ref)

    def attn_chunk(c, carry):
        kk = keys_ref[c]
        tie = kk == thr
        tie_b = jnp.where(tie, 1.0, 0.0).astype(BF16)
        before = _dot(ltri_ref[...], tie_b) + ties_ref[...]
        tie_bias = jnp.where(before < take, 0.0, NEG_BIG)
        bias_ref[...] = jnp.where(kk > thr, 0.0, jnp.where(tie, tie_bias, NEG_BIG))
        ties_ref[...] += jnp.sum(tie_b.astype(F32), axis=0, keepdims=True)
        start = pl.multiple_of(c * TK, TK)

        def logits(h):
            kc = k_ref[pl.ds(start, TK), (h // 2) * 128:(h // 2 + 1) * 128]
            att_ref[h] = _dot(kc, qz_ref[h * 128:(h + 1) * 128, :]) + bias_ref[...]

        for h in range(min(LOOKAHEAD, ATTN_HEADS)):
            logits(h)
        for h in range(ATTN_HEADS):
            if h % 2 == 0:
                for hh in range(h + LOOKAHEAD, min(h + LOOKAHEAD + 2, ATTN_HEADS)):
                    logits(hh)
            m_old = m_ref[h:h + 1, :]
            m_new = jnp.maximum(m_old, jnp.max(att_ref[h], axis=0, keepdims=True))
            p = jnp.exp2(att_ref[h] - m_new)
            alpha = jnp.exp2(m_old - m_new)
            l_ref[h:h + 1, :] = alpha * l_ref[h:h + 1, :] + jnp.sum(p, axis=0, keepdims=True)
            pv = _dot(v_ref[c, h * 64:(h + 1) * 64, :], p.astype(BF16))
            acc_ref[h * 64:(h + 1) * 64, :] = alpha * acc_ref[h * 64:(h + 1) * 64, :] + pv
            m_ref[h:h + 1, :] = m_new
        return carry

    lax.fori_loop(0, nc, attn_chunk, 0)

    for h in range(ATTN_HEADS):
        o_ref[h * 64:(h + 1) * 64, :] = (
            acc_ref[h * 64:(h + 1) * 64, :] / l_ref[h:h + 1, :]).astype(BF16)


def _dsa(qz, qiT, wiT, k, ki, vTc):
    B, _, L = qiT.shape
    n_sel = min(TOPK_MAX, L // 4)
    nck = L // TK
    ltri = jnp.tril(jnp.ones((TK, TK), BF16), -1)
    return pl.pallas_call(
        functools.partial(_dsa_kernel, n_sel),
        grid=(B, L // TQ),
        in_specs=[
            pl.BlockSpec((None, ATTN_HEADS * 128, TQ), lambda b, q: (b, 0, q)),
            pl.BlockSpec((None, D_ATTN, TQ), lambda b, q: (b, 0, q)),
            pl.BlockSpec((None, IDX_HEADS, TQ), lambda b, q: (b, 0, q)),
            pl.BlockSpec((None, L, D_ATTN), lambda b, q: (b, 0, 0)),
            pl.BlockSpec((None, L, IDX_DIM), lambda b, q: (b, 0, 0)),
            pl.BlockSpec((None, nck, D_ATTN, TK), lambda b, q: (b, 0, 0, 0)),
            pl.BlockSpec((TK, TK), lambda b, q: (0, 0)),
        ],
        out_specs=pl.BlockSpec((None, D_ATTN, TQ), lambda b, q: (b, 0, q)),
        out_shape=jax.ShapeDtypeStruct((B, D_ATTN, L), BF16),
        scratch_shapes=[
            pltpu.VMEM((nck, TK, TQ), I32),
            pltpu.VMEM((D_ATTN, TQ), F32),
            pltpu.VMEM((ATTN_HEADS, TQ), F32),
            pltpu.VMEM((ATTN_HEADS, TQ), F32),
            pltpu.VMEM((TK, TQ), F32),
            pltpu.VMEM((ATTN_HEADS, TK, TQ), F32),
            pltpu.VMEM((1, TQ), F32),
        ],
        compiler_params=pltpu.CompilerParams(
            dimension_semantics=("parallel", "arbitrary"), vmem_limit_bytes=VMEM_LIMIT),
        name="dsa",
    )(qz, qiT, wiT, k, ki, vTc, ltri)


def _ln_rows(y, g, b):
    mu = jnp.mean(y, axis=0, keepdims=True)
    var = jnp.mean(jnp.square(y - mu), axis=0, keepdims=True)
    return (y - mu) * lax.rsqrt(var + LN_EPS) * g + b


def _mix_kernel(a_ref, c_ref, x_ref, wa_ref, wc_ref, g_ref, b_ref, o_ref, ob_ref):
    mix = _dot(wa_ref[...], a_ref[...]) + _dot(wc_ref[...], c_ref[...])
    y = DEEPNORM_ALPHA * x_ref[...].T + mix
    x1 = _ln_rows(y, _lane_tile(g_ref[...], TN), _lane_tile(b_ref[...], TN))
    o_ref[...] = x1
    ob_ref[...] = x1.astype(BF16)


def _mix(attnT, convT, x, woa, woc, g, b):
    B, L, _ = x.shape
    nl = L // TN
    full = lambda shape: pl.BlockSpec(shape, lambda b_, i: (0,) * len(shape))
    return pl.pallas_call(
        _mix_kernel,
        grid=(B, nl),
        in_specs=[
            pl.BlockSpec((None, D_ATTN, TN), lambda b_, i: (b_, 0, i)),
            pl.BlockSpec((None, D_CONV, TN), lambda b_, i: (b_, 0, i)),
            pl.BlockSpec((None, TN, D_MODEL), lambda b_, i: (b_, i, 0)),
            full(woa.shape), full(woc.shape), full(g.shape), full(b.shape),
        ],
        out_specs=[
            pl.BlockSpec((D_MODEL, TN), lambda b_, i: (0, b_ * nl + i)),
            pl.BlockSpec((D_MODEL, TN), lambda b_, i: (0, b_ * nl + i)),
        ],
        out_shape=[
            jax.ShapeDtypeStruct((D_MODEL, B * L), F32),
            jax.ShapeDtypeStruct((D_MODEL, B * L), BF16),
        ],
        compiler_params=pltpu.CompilerParams(
            dimension_semantics=("parallel", "parallel"), vmem_limit_bytes=VMEM_LIMIT),
        name="mix_ln1",
    )(attnT, convT, x, woa, woc, g, b)


def _extract_top(s, n_rows, steps, exact_ties, want_rank):
    T = s.shape[1]
    orig = s
    iota = lax.broadcasted_iota(I32, (n_rows, T), 0).astype(F32)
    rank = jnp.full((n_rows, T), float(steps), F32) if want_rank else None
    vals = []
    for a in range(steps):
        cur = jnp.max(s, axis=0, keepdims=True)
        hit = s == cur
        if exact_ties:
            first = jnp.min(jnp.where(hit, iota, float(n_rows)), axis=0, keepdims=True)
            hit = iota == first
        if want_rank:
            rank = jnp.where(hit, float(a), rank)
        s = jnp.where(hit, -jnp.inf, s)
        vals.append(cur)
    member = s != orig
    removed = jnp.sum(jnp.where(member, 1.0, 0.0), axis=0, keepdims=True)
    return rank, vals, member, removed


def _pack_words(x):
    bits = pltpu.bitcast(x.astype(BF16).astype(F32), I32)
    return bits | lax.shift_right_logical(bits, 16)


def _route_head(h, qT, s1_ref, s2_ref, r2_ref, c_ref, e1_ref, e2_ref, exact_ties):
    neg_inf = jnp.full((8, TR), -jnp.inf, F32)
    sub8 = lax.broadcasted_iota(I32, (8, TR), 0)
    q1 = qT[h * 128:h * 128 + PEER_HALF].astype(BF16)
    q2 = qT[h * 128 + PEER_HALF:(h + 1) * 128].astype(BF16)
    s1 = _dot(s1_ref[h], q1)
    s2 = _dot(s2_ref[h], q2)
    r1, v1, in1, n1 = _extract_top(s1, PEER_NKEYS, PEER_TOPK, exact_ties, want_rank=exact_ties)
    r2, v2, in2, n2 = _extract_top(s2, PEER_NKEYS, PEER_TOPK, exact_ties, want_rank=True)
    v1m = jnp.concatenate(v1, axis=0)
    v2m = jnp.concatenate(v2, axis=0)
    blocks = [v1[0] + v2m[:8], v1[0] + v2m[8:]]
    for a in range(1, 8):
        blocks.append(jnp.where(sub8 < PEER_TOPK // (a + 1), v1[a] + v2m[:8], neg_inf))
    blocks.append(v1m[8:] + v2[0])
    cand = jnp.concatenate(blocks, axis=0)
    _, _, chosen, n3 = _extract_top(cand, cand.shape[0], PEER_TOPK, exact_ties, want_rank=False)
    top = v1[0] + v2[0]
    z = jnp.sum(jnp.where(chosen, jnp.exp(cand - top), 0.0), axis=0, keepdims=True)
    chosen_f = chosen.astype(F32)
    per_a = [jnp.sum(chosen_f[:16], axis=0, keepdims=True)]
    for a in range(1, 8):
        per_a.append(jnp.sum(chosen_f[8 + 8 * a:16 + 8 * a], axis=0, keepdims=True))
    for a in range(8, PEER_TOPK):
        per_a.append(chosen_f[64 + a:65 + a])
    cnt = jnp.zeros((PEER_NKEYS, TR), F32)
    for a in range(PEER_TOPK):
        is_a = (r1 == float(a)) if exact_ties else (s1 == v1[a])
        cnt = jnp.where(is_a, per_a[a], cnt)
    r2_ref[h] = r2.astype(BF16)
    c_ref[h] = _pack_words(cnt)
    e1_ref[h] = _pack_words(jnp.where(in1, jnp.exp(s1 - v1[0]), 0.0) / z)
    e2_ref[h] = jnp.where(in2, jnp.exp(s2 - v2[0]), 0.0).astype(BF16)
    return jnp.maximum(jnp.maximum(n1, n2), n3)


def _route_kernel(x_ref, wq_ref, s1_ref, s2_ref, r2_ref, c_ref, e1_ref, e2_ref, q_scr):
    q_scr[...] = _dot(wq_ref[...], x_ref[...])
    refs = (s1_ref, s2_ref, r2_ref, c_ref, e1_ref, e2_ref)
    worst = [_route_head(h, q_scr, *refs, exact_ties=False) for h in range(PEER_HEADS)]
    for h in range(PEER_HEADS):
        @pl.when(jnp.max(worst[h]) > float(PEER_TOPK))
        def _():
            _route_head(h, q_scr, *refs, exact_ties=True)


def _route(x1b, wqT, sub1, sub2):
    _, N = x1b.shape
    full = lambda shape: pl.BlockSpec(shape, lambda i: (0,) * len(shape))
    spec = pl.BlockSpec((PEER_HEADS, PEER_NKEYS, TR), lambda i: (0, 0, i))
    i32 = jax.ShapeDtypeStruct((PEER_HEADS, PEER_NKEYS, N), I32)
    b16 = jax.ShapeDtypeStruct((PEER_HEADS, PEER_NKEYS, N), BF16)
    return pl.pallas_call(
        _route_kernel,
        grid=(N // TR,),
        in_specs=[pl.BlockSpec((D_MODEL, TR), lambda i: (0, i)),
                  full(wqT.shape), full(sub1.shape), full(sub2.shape)],
        out_specs=[spec, spec, spec, spec],
        out_shape=[b16, i32, i32, b16],
        scratch_shapes=[pltpu.VMEM((D_MODEL, TR), F32)],
        compiler_params=pltpu.CompilerParams(
            dimension_semantics=("parallel",), vmem_limit_bytes=VMEM_LIMIT),
        name="peer_route",
    )(x1b, wqT, sub1, sub2)


def _peer_kernel(xb_ref, x_ref, r2_ref, c_ref, e1_ref, e2_ref, u_ref, vt_ref, g_ref, b_ref,
                 o_ref, acc_ref, hh_ref, gh_ref):
    c = pl.program_id(1)

    @pl.when(c == 0)
    def _():
        acc_ref[...] = jnp.zeros_like(acc_ref)

    def packed_rows(words):
        return pltpu.bitcast(jnp.broadcast_to(words, (GROWS // 2, TB)), BF16)

    zero = jnp.zeros((), BF16)
    per_piece = PIECE // PEER_NKEYS

    def gates(k):
        for s in range(k * per_piece, (k + 1) * per_piece):
            for rb in range(PEER_NKEYS // GROWS):
                rs = slice(rb * GROWS, (rb + 1) * GROWS)
                g = None
                for h in range(PEER_HEADS):
                    cb = packed_rows(c_ref[h, s:s + 1, :])
                    eb = packed_rows(e1_ref[h, s:s + 1, :])
                    t = jnp.where(r2_ref[h, rs, :] < cb, e2_ref[h, rs, :], zero) * eb
                    g = t if g is None else g + t
                gh_ref[s * PEER_NKEYS + rb * GROWS:s * PEER_NKEYS + (rb + 1) * GROWS, :] = g

    def up(k):
        ps = slice(k * PIECE, (k + 1) * PIECE)
        hh_ref[ps, :] = _dot(u_ref[ps, :], xb_ref[...])

    def activate(k):
        ps = slice(k * PIECE, (k + 1) * PIECE)
        hh = hh_ref[ps, :]
        act = (hh * (1.0 + lax.erf(hh * (2.0 ** 0.5)))).astype(BF16)
        gh_ref[ps, :] = gh_ref[ps, :] * act

    def down(k0, k1):
        ps = slice(k0 * PIECE, k1 * PIECE)
        acc_ref[...] += _dot(vt_ref[:, ps], gh_ref[ps, :])

    n_piece = EC // PIECE
    down_every = max(n_piece // 2, 1)
    up(0)
    for k in range(n_piece):
        if k + 1 < n_piece:
            up(k + 1)
        gates(k)
        activate(k)
        if (k + 1) % down_every == 0:
            down(k + 1 - down_every, k + 1)

    @pl.when(c == pl.num_programs(1) - 1)
    def _():
        y = DEEPNORM_ALPHA * x_ref[...] + acc_ref[...]
        x2 = _ln_rows(y, _lane_tile(g_ref[...], TB), _lane_tile(b_ref[...], TB))
        o_ref[...] = x2.T


def _peer(x1b, x1, r2, cnt, e1, e2, u, vt, g, b):
    _, N = x1.shape
    rows = EC // PEER_NKEYS
    full = lambda shape: pl.BlockSpec(shape, lambda j, c: (0,) * len(shape))
    tok = pl.BlockSpec((PEER_HEADS, PEER_NKEYS, TB), lambda j, c: (0, 0, j))
    row = pl.BlockSpec((PEER_HEADS, rows, TB), lambda j, c: (0, c, j))
    return pl.pallas_call(
        _peer_kernel,
        grid=(N // TB, PEER_NEXPERTS // EC),
        in_specs=[
            pl.BlockSpec((D_MODEL, TB), lambda j, c: (0, j)),
            pl.BlockSpec((D_MODEL, TB), lambda j, c: (0, j)),
            tok, row, row, tok,
            pl.BlockSpec((EC, D_MODEL), lambda j, c: (c, 0)),
            pl.BlockSpec((D_MODEL, EC), lambda j, c: (0, c)),
            full(g.shape), full(b.shape),
        ],
        out_specs=pl.BlockSpec((TB, D_MODEL), lambda j, c: (j, 0)),
        out_shape=jax.ShapeDtypeStruct((N, D_MODEL), F32),
        scratch_shapes=[pltpu.VMEM((D_MODEL, TB), F32),
                        pltpu.VMEM((EC, TB), F32),
                        pltpu.VMEM((EC, TB), BF16)],
        compiler_params=pltpu.CompilerParams(
            dimension_semantics=("parallel", "arbitrary"), vmem_limit_bytes=VMEM_LIMIT),
        name="peer_dense",
    )(x1b, x1, r2, cnt, e1, e2, u, vt, g, b)


def _col_bcast(v, n):
    return jnp.broadcast_to(v.astype(F32)[:, None], (v.shape[0], n))


def kernel(x, w_in, idx_k_ln_g, idx_k_ln_b, conv_dw_w, conv_dw_b, conv_ln_g, conv_ln_b,
           w_out, ln1_g, ln1_b, peer_w_q, peer_sub_keys1, peer_sub_keys2, peer_u, peer_v,
           ln2_g, ln2_b):
    B, L, _ = x.shape
    assert L % TM == 0 and L % TQ == 0 and TQ == TK and (B * L) % TB == 0
    l = 0

    w = w_in[l]
    wT = w.T.astype(BF16)
    wm = jnp.concatenate([wT[OFF_Q:OFF_KI]], axis=0)
    ws = jnp.concatenate([wT[OFF_KI:OFF_CONV],
                          jnp.zeros((LANES - IDX_DIM - IDX_HEADS, D_MODEL), BF16)], axis=0)
    wc = w[:, OFF_CONV:].astype(BF16)
    woT = w_out[l].T.astype(BF16)
    woa, woc = woT[:, :D_ATTN], woT[:, D_ATTN:]
    wqT = peer_w_q[l].T.astype(BF16)
    sub1 = peer_sub_keys1[l].astype(BF16)
    sub2 = peer_sub_keys2[l].astype(BF16)
    u = (0.5 * peer_u[l]).astype(BF16)
    vt = peer_v[l].T.astype(BF16)

    pos = jnp.arange(L, dtype=F32)
    inv_freq = ROPE_THETA ** (-jnp.arange(0, HEAD_DIM, 2, dtype=F32) / HEAD_DIM)
    ang = inv_freq[:, None] * pos[None, :]
    cosT = jnp.concatenate([jnp.cos(ang), jnp.cos(ang)], axis=0)
    sinT = jnp.concatenate([jnp.sin(ang), jnp.sin(ang)], axis=0)

    qz, qiT, vTc, wiT, k, ki, h = _proj(
        x, wm, ws, wc, cosT, sinT, _col_bcast(idx_k_ln_g[l], TM), _col_bcast(idx_k_ln_b[l], TM))

    dw = jnp.concatenate([conv_dw_w[l], jnp.zeros((1, D_CONV), F32)], axis=0)
    convT = _conv(h, dw, conv_dw_b[l][None, :], conv_ln_g[l][None, :], conv_ln_b[l][None, :])

    attnT = _dsa(qz, qiT, wiT, k, ki, vTc)

    x1, x1b = _mix(attnT, convT, x, woa, woc,
                   _col_bcast(ln1_g[l], LANES), _col_bcast(ln1_b[l], LANES))

    r2, cnt, e1, e2 = _route(x1b, wqT, sub1, sub2)
    out = _peer(x1b, x1, r2, cnt, e1, e2, u, vt,
                _col_bcast(ln2_g[l], LANES), _col_bcast(ln2_b[l], LANES))
    return out.reshape(B, L, D_MODEL)
```

```python
import functools

import jax
import jax.numpy as jnp
from jax import lax
from jax.experimental import pallas as pl
from jax.experimental.pallas import tpu as pltpu

F32 = jnp.float32
BF16 = jnp.bfloat16
I32 = jnp.int32
I16 = jnp.int16

D_MODEL = 1024
ATTN_HEADS = 8
HEAD_DIM = 64
D_ATTN = ATTN_HEADS * HEAD_DIM
D_CONV = 512
CONV_WIDTH = 31
IDX_HEADS = 8
IDX_DIM = 64
TOPK_MAX = 256
ROPE_THETA = 10000.0
PEER_HEADS = 8
PEER_NKEYS = 128
PEER_NEXPERTS = PEER_NKEYS * PEER_NKEYS
PEER_HALF = 64
PEER_TOPK = 16
LN_EPS = 1e-5
DEPTH = 1
DEEPNORM_ALPHA = (2.0 * DEPTH) ** 0.25

OFF_Q = 0
OFF_K = OFF_Q + D_ATTN
OFF_V = OFF_K + D_ATTN
OFF_QI = OFF_V + D_ATTN
OFF_KI = OFF_QI + IDX_HEADS * IDX_DIM
OFF_WI = OFF_KI + IDX_DIM
OFF_CONV = OFF_WI + IDX_HEADS

LANES = 128
SUBLANES = 8
PACK = 16
VMEM_LIMIT = 56 * 1024 * 1024

TM = 512
TC = 512
HALO = 32
TQ = 256
TK = 256
LOOKAHEAD = 4
TN = 512
TR = 256
TB = 512
EC = 1024
PIECE = 256
GROWS = 128

LOG2E = 1.4426950408889634
Q_SCALE = HEAD_DIM ** -0.5 * LOG2E
NEG_BIG = -1e30
INT_MIN = -2 ** 31
I16_MIN = -2 ** 15
I16_MAX = 2 ** 15 - 1


def _nt_dot(a, b):
    return lax.dot_general(a, b, (((1,), (1,)), ((), ())), preferred_element_type=F32)


def _dot(a, b):
    return jnp.dot(a, b, preferred_element_type=F32)


def _lane_tile(a, n):
    reps = n // LANES
    return a if reps == 1 else jnp.concatenate([a] * reps, axis=1)


def _rope_rows(z, cos, sin):
    x1, x2 = z[:32], z[32:]
    top = x1 * cos[:32] - x2 * sin[:32]
    bot = x2 * cos[32:] + x1 * sin[32:]
    return top, bot


def _proj_kernel(x_ref, wm_ref, ws_ref, wc_ref, cos_ref, sin_ref, kig_ref, kib_ref,
                 qz_ref, qi_ref, v_ref, wi_ref, k_ref, ki_ref, h_ref, kt_scr):
    xb = x_ref[...].astype(BF16)
    cos = cos_ref[...]
    sin = sin_ref[...]
    pT = _nt_dot(wm_ref[...], xb)

    zeros64 = jnp.zeros((HEAD_DIM, TM), BF16)
    for h in range(ATTN_HEADS):
        top, bot = _rope_rows(pT[OFF_Q + h * 64:OFF_Q + (h + 1) * 64], cos, sin)
        qh = (jnp.concatenate([top, bot], axis=0) * Q_SCALE).astype(BF16)
        if h % 2 == 0:
            qz_ref[h * 128:h * 128 + 64, :] = qh
            qz_ref[h * 128 + 64:(h + 1) * 128, :] = zeros64
        else:
            qz_ref[h * 128:h * 128 + 64, :] = zeros64
            qz_ref[h * 128 + 64:(h + 1) * 128, :] = qh
        top, bot = _rope_rows(pT[OFF_K + h * 64:OFF_K + (h + 1) * 64], cos, sin)
        kt_scr[h * 64:h * 64 + 32, :] = top
        kt_scr[h * 64 + 32:(h + 1) * 64, :] = bot
        top, bot = _rope_rows(pT[OFF_QI + h * 64:OFF_QI + (h + 1) * 64], cos, sin)
        qi_ref[h * 64:h * 64 + 32, :] = top.astype(BF16)
        qi_ref[h * 64 + 32:(h + 1) * 64, :] = bot.astype(BF16)
    k_ref[...] = kt_scr[...].T.astype(BF16)

    vT = pT[OFF_V:OFF_V + D_ATTN].astype(BF16)
    for j in range(TM // TK):
        v_ref[j] = vT[:, j * TK:(j + 1) * TK]

    sT = _nt_dot(ws_ref[...], xb)
    z = sT[:IDX_DIM]
    mu = jnp.mean(z, axis=0, keepdims=True)
    var = jnp.mean(jnp.square(z - mu), axis=0, keepdims=True)
    y = (z - mu) * lax.rsqrt(var + LN_EPS) * kig_ref[...] + kib_ref[...]
    top, bot = _rope_rows(y, cos, sin)
    kiT = jnp.concatenate([top, bot, jnp.zeros((LANES - IDX_DIM, TM), F32)], axis=0)
    ki_ref[...] = kiT.T[:, :IDX_DIM].astype(BF16)
    wi_ref[...] = sT[IDX_DIM:IDX_DIM + IDX_HEADS] * (IDX_HEADS ** -0.5)

    a = _dot(xb, wc_ref[...])
    h_ref[...] = a[:, :D_CONV] * jax.nn.sigmoid(a[:, D_CONV:])


def _proj(x, wm, ws, wc, cosT, sinT, kig, kib):
    B, L, _ = x.shape
    grid = (B, L // TM)
    full = lambda shape: pl.BlockSpec(shape, lambda b, i: (0,) * len(shape))
    return pl.pallas_call(
        _proj_kernel,
        grid=grid,
        in_specs=[
            pl.BlockSpec((None, TM, D_MODEL), lambda b, i: (b, i, 0)),
            full(wm.shape), full(ws.shape), full(wc.shape),
            pl.BlockSpec((HEAD_DIM, TM), lambda b, i: (0, i)),
            pl.BlockSpec((HEAD_DIM, TM), lambda b, i: (0, i)),
            full(kig.shape), full(kib.shape),
        ],
        out_specs=[
            pl.BlockSpec((None, ATTN_HEADS * 128, TM), lambda b, i: (b, 0, i)),
            pl.BlockSpec((None, D_ATTN, TM), lambda b, i: (b, 0, i)),
            pl.BlockSpec((None, TM // TK, D_ATTN, TK), lambda b, i: (b, i, 0, 0)),
            pl.BlockSpec((None, IDX_HEADS, TM), lambda b, i: (b, 0, i)),
            pl.BlockSpec((None, TM, D_ATTN), lambda b, i: (b, i, 0)),
            pl.BlockSpec((None, TM, IDX_DIM), lambda b, i: (b, i, 0)),
            pl.BlockSpec((None, TM, D_CONV), lambda b, i: (b, i, 0)),
        ],
        out_shape=[
            jax.ShapeDtypeStruct((B, ATTN_HEADS * 128, L), BF16),
            jax.ShapeDtypeStruct((B, D_ATTN, L), BF16),
            jax.ShapeDtypeStruct((B, L // TK, D_ATTN, TK), BF16),
            jax.ShapeDtypeStruct((B, IDX_HEADS, L), F32),
            jax.ShapeDtypeStruct((B, L, D_ATTN), BF16),
            jax.ShapeDtypeStruct((B, L, IDX_DIM), BF16),
            jax.ShapeDtypeStruct((B, L, D_CONV), F32),
        ],
        scratch_shapes=[pltpu.VMEM((D_ATTN, TM), F32)],
        compiler_params=pltpu.CompilerParams(
            dimension_semantics=("parallel", "parallel"), vmem_limit_bytes=VMEM_LIMIT),
        name="proj",
    )(x, wm, ws, wc, cosT, sinT, kig, kib)


def _conv_kernel(h_ref, halo_ref, w_ref, b_ref, g_ref, beta_ref, o_ref, buf, shifted, acc):
    i = pl.program_id(1)
    rows = HALO + TC
    buf[0:HALO, :] = jnp.where(i > 0, halo_ref[...], 0.0)
    buf[HALO:rows, :] = h_ref[...]
    buf[rows:rows + SUBLANES, :] = jnp.zeros((SUBLANES, D_CONV), F32)
    for r in range(SUBLANES):
        shifted[r] = buf[r:r + rows, :]
    base = HALO - (CONV_WIDTH - 1)
    for c in range(D_CONV // LANES):
        cs = slice(c * LANES, (c + 1) * LANES)
        for rb in range(TC // LANES):
            a = jnp.zeros((LANES, LANES), F32)
            for j in range(CONV_WIDTH):
                phase = (base + j) % SUBLANES
                s = base + j - phase + rb * LANES
                a = a + shifted[phase, s:s + LANES, cs] * w_ref[j:j + 1, cs]
            acc[rb * LANES:(rb + 1) * LANES, cs] = a
    y = acc[...] + b_ref[...]
    mu = jnp.mean(y, axis=-1, keepdims=True)
    var = jnp.mean(jnp.square(y - mu), axis=-1, keepdims=True)
    y = (y - mu) * lax.rsqrt(var + LN_EPS) * g_ref[...] + beta_ref[...]
    y = y * jax.nn.sigmoid(y)
    o_ref[...] = y.T.astype(BF16)


def _conv(h, w, b, g, beta):
    B, L, _ = h.shape
    full = lambda shape: pl.BlockSpec(shape, lambda b_, i: (0,) * len(shape))
    return pl.pallas_call(
        _conv_kernel,
        grid=(B, L // TC),
        in_specs=[
            pl.BlockSpec((None, TC, D_CONV), lambda b_, i: (b_, i, 0)),
            pl.BlockSpec((None, HALO, D_CONV),
                         lambda b_, i: (b_, jnp.maximum(i * (TC // HALO) - 1, 0), 0)),
            full(w.shape), full(b.shape), full(g.shape), full(beta.shape),
        ],
        out_specs=pl.BlockSpec((None, D_CONV, TC), lambda b_, i: (b_, 0, i)),
        out_shape=jax.ShapeDtypeStruct((B, D_CONV, L), BF16),
        scratch_shapes=[pltpu.VMEM((HALO + TC + SUBLANES, D_CONV), F32),
                        pltpu.VMEM((SUBLANES, HALO + TC, D_CONV), F32),
                        pltpu.VMEM((TC, D_CONV), F32)],
        compiler_params=pltpu.CompilerParams(
            dimension_semantics=("parallel", "parallel"), vmem_limit_bytes=VMEM_LIMIT),
        name="conv",
    )(h, h, w, b, g, beta)


def _dsa_kernel(n_sel, qz_ref, qi_ref, wi_ref, k_ref, ki_ref, v_ref, ltri_ref, o_ref,
                keys_ref, hi_ref, lo_ref, acc_ref, m_ref, l_ref, bias_ref, att_ref, ties_ref):
    qb = pl.program_id(1)
    nc = qb + 1
    rows = lax.broadcasted_iota(I32, (TK, TQ), 0)
    cols = lax.broadcasted_iota(I32, (TK, TQ), 1)

    def score_chunk(c, carry):
        kic = ki_ref[pl.ds(pl.multiple_of(c * TK, TK), TK), :]
        s = jnp.zeros((TK, TQ), F32)
        for h in range(IDX_HEADS):
            z = _dot(kic, qi_ref[h * IDX_DIM:(h + 1) * IDX_DIM, :])
            s = s + wi_ref[h:h + 1, :] * jnp.maximum(z, 0.0)
        s = jnp.where(s == 0.0, 0.0, s)
        bits = pltpu.bitcast(s, I32)
        key = jnp.where(bits < 0, bits ^ jnp.int32(0x7FFFFFFF), bits)
        valid = (c * TK + rows) <= (qb * TQ + cols)
        key = jnp.where(valid, key, jnp.int32(INT_MIN))
        keys_ref[c] = key
        hi_ref[c] = lax.shift_right_arithmetic(key, 16).astype(I16)
        return carry

    lax.fori_loop(0, nc, score_chunk, 0)

    n_pair = (nc + 1) // 2

    @pl.when(nc % 2 == 1)
    def _():
        keys_ref[nc] = jnp.full((TK, TQ), INT_MIN, I32)
        hi_ref[nc] = jnp.full((TK, TQ), I16_MIN, I16)

    def count_ge(cand):
        def body(i, cnt):
            a = jnp.sum((keys_ref[2 * i] >= cand).astype(I32), axis=0, keepdims=True)
            b = jnp.sum((keys_ref[2 * i + 1] >= cand).astype(I32), axis=0, keepdims=True)
            return cnt + (a + b)
        return lax.fori_loop(0, n_pair, body, jnp.zeros((1, TQ), I32))

    def count_ge16(ref, cand):
        c16 = cand.astype(I16)
        one, none = jnp.ones((), I16), jnp.zeros((), I16)

        def col_count(x):
            hits = jnp.where(x >= c16, one, none)
            part = hits[0:PACK]
            for r in range(1, TK // PACK):
                part = part + hits[r * PACK:(r + 1) * PACK]
            return jnp.sum(part.astype(I32), axis=0, keepdims=True)

        def body(i, cnt):
            return cnt + (col_count(ref[2 * i]) + col_count(ref[2 * i + 1]))
        return lax.fori_loop(0, n_pair, body, jnp.zeros((1, TQ), I32))

    def search16(ref, need):
        def bit_step(i, prefix):
            cand = prefix + lax.shift_left(jnp.int32(1), 15 - i)
            return jnp.where(count_ge16(ref, cand) >= need, cand, prefix)
        return lax.fori_loop(0, 16, bit_step, jnp.full((1, TQ), I16_MIN, I32))

    t_hi = search16(hi_ref, n_sel)
    above = jnp.where(t_hi == I16_MAX, 0, count_ge16(hi_ref, jnp.minimum(t_hi + 1, I16_MAX)))

    def low_chunk(c, carry):
        key = keys_ref[c]
        low = (key & jnp.int32(0xFFFF)) + jnp.int32(I16_MIN)
        in_bucket = lax.shift_right_arithmetic(key, 16) == t_hi
        lo_ref[c] = jnp.where(in_bucket, low, jnp.int32(I16_MIN)).astype(I16)
        return carry

    lax.fori_loop(0, 2 * n_pair, low_chunk, 0)
    t_lo = search16(lo_ref, n_sel - above)
    prefix = lax.shift_left(t_hi, 16) + (t_lo - jnp.int32(I16_MIN))
    thr = jnp.maximum(prefix, jnp.int32(INT_MIN + 1))
    take = (n_sel - count_ge(thr + 1)).astype(F32)

    acc_ref[...] = jnp.zeros_like(acc_ref)
    m_ref[...] = jnp.full_like(m_ref, NEG_BIG)
    l_ref[...] = jnp.zeros_like(l_ref)
    ties_ref[...] = jnp.zeros_like(ties_ref)

    def attn_chunk(c, carry):
        kk = keys_ref[c]
        tie = kk == thr
        tie_b = jnp.where(tie, 1.0, 0.0).astype(BF16)
        before = _dot(ltri_ref[...], tie_b) + ties_ref[...]
        tie_bias = jnp.where(before < take, 0.0, NEG_BIG)
        bias_ref[...] = jnp.where(kk > thr, 0.0, jnp.where(tie, tie_bias, NEG_BIG))
        ties_ref[...] += jnp.sum(tie_b.astype(F32), axis=0, keepdims=True)
        start = pl.multiple_of(c * TK, TK)

        def logits(h):
            kc = k_ref[pl.ds(start, TK), (h // 2) * 128:(h // 2 + 1) * 128]
            att_ref[h] = _dot(kc, qz_ref[h * 128:(h + 1) * 128, :]) + bias_ref[...]

        for h in range(min(LOOKAHEAD, ATTN_HEADS)):
            logits(h)
        for h in range(ATTN_HEADS):
            if h % 2 == 0:
                for hh in range(h + LOOKAHEAD, min(h + LOOKAHEAD + 2, ATTN_HEADS)):
                    logits(hh)
            m_old = m_ref[h:h + 1, :]
            m_new = jnp.maximum(m_old, jnp.max(att_ref[h], axis=0, keepdims=True))
            p = jnp.exp2(att_ref[h] - m_new)
            alpha = jnp.exp2(m_old - m_new)
            l_ref[h:h + 1, :] = alpha * l_ref[h:h + 1, :] + jnp.sum(p, axis=0, keepdims=True)
            pv = _dot(v_ref[c, h * 64:(h + 1) * 64, :], p.astype(BF16))
            acc_ref[h * 64:(h + 1) * 64, :] = alpha * acc_ref[h * 64:(h + 1) * 64, :] + pv
            m_ref[h:h + 1, :] = m_new
        return carry

    lax.fori_loop(0, nc, attn_chunk, 0)

    for h in range(ATTN_HEADS):
        o_ref[h * 64:(h + 1) * 64, :] = (
            acc_ref[h * 64:(h + 1) * 64, :] / l_ref[h:h + 1, :]).astype(BF16)


def _dsa(qz, qiT, wiT, k, ki, vTc):
    B, _, L = qiT.shape
    n_sel = min(TOPK_MAX, L // 4)
    nck = L // TK
    ltri = jnp.tril(jnp.ones((TK, TK), BF16), -1)
    return pl.pallas_call(
        functools.partial(_dsa_kernel, n_sel),
        grid=(B, L // TQ),
        in_specs=[
            pl.BlockSpec((None, ATTN_HEADS * 128, TQ), lambda b, q: (b, 0, q)),
            pl.BlockSpec((None, D_ATTN, TQ), lambda b, q: (b, 0, q)),
            pl.BlockSpec((None, IDX_HEADS, TQ), lambda b, q: (b, 0, q)),
            pl.BlockSpec((None, L, D_ATTN), lambda b, q: (b, 0, 0)),
            pl.BlockSpec((None, L, IDX_DIM), lambda b, q: (b, 0, 0)),
            pl.BlockSpec((None, nck, D_ATTN, TK), lambda b, q: (b, 0, 0, 0)),
            pl.BlockSpec((TK, TK), lambda b, q: (0, 0)),
        ],
        out_specs=pl.BlockSpec((None, D_ATTN, TQ), lambda b, q: (b, 0, q)),
        out_shape=jax.ShapeDtypeStruct((B, D_ATTN, L), BF16),
        scratch_shapes=[
            pltpu.VMEM((nck, TK, TQ), I32),
            pltpu.VMEM((nck, TK, TQ), I16),
            pltpu.VMEM((nck, TK, TQ), I16),
            pltpu.VMEM((D_ATTN, TQ), F32),
            pltpu.VMEM((ATTN_HEADS, TQ), F32),
            pltpu.VMEM((ATTN_HEADS, TQ), F32),
            pltpu.VMEM((TK, TQ), F32),
            pltpu.VMEM((ATTN_HEADS, TK, TQ), F32),
            pltpu.VMEM((1, TQ), F32),
        ],
        compiler_params=pltpu.CompilerParams(
            dimension_semantics=("parallel", "arbitrary"), vmem_limit_bytes=VMEM_LIMIT),
        name="dsa",
    )(qz, qiT, wiT, k, ki, vTc, ltri)


def _ln_rows(y, g, b):
    mu = jnp.mean(y, axis=0, keepdims=True)
    var = jnp.mean(jnp.square(y - mu), axis=0, keepdims=True)
    return (y - mu) * lax.rsqrt(var + LN_EPS) * g + b


def _mix_kernel(a_ref, c_ref, x_ref, wa_ref, wc_ref, g_ref, b_ref, o_ref, ob_ref):
    mix = _dot(wa_ref[...], a_ref[...]) + _dot(wc_ref[...], c_ref[...])
    y = DEEPNORM_ALPHA * x_ref[...].T + mix
    x1 = _ln_rows(y, _lane_tile(g_ref[...], TN), _lane_tile(b_ref[...], TN))
    o_ref[...] = x1
    ob_ref[...] = x1.astype(BF16)


def _mix(attnT, convT, x, woa, woc, g, b):
    B, L, _ = x.shape
    nl = L // TN
    full = lambda shape: pl.BlockSpec(shape, lambda b_, i: (0,) * len(shape))
    return pl.pallas_call(
        _mix_kernel,
        grid=(B, nl),
        in_specs=[
            pl.BlockSpec((None, D_ATTN, TN), lambda b_, i: (b_, 0, i)),
            pl.BlockSpec((None, D_CONV, TN), lambda b_, i: (b_, 0, i)),
            pl.BlockSpec((None, TN, D_MODEL), lambda b_, i: (b_, i, 0)),
            full(woa.shape), full(woc.shape), full(g.shape), full(b.shape),
        ],
        out_specs=[
            pl.BlockSpec((D_MODEL, TN), lambda b_, i: (0, b_ * nl + i)),
            pl.BlockSpec((D_MODEL, TN), lambda b_, i: (0, b_ * nl + i)),
        ],
        out_shape=[
            jax.ShapeDtypeStruct((D_MODEL, B * L), F32),
            jax.ShapeDtypeStruct((D_MODEL, B * L), BF16),
        ],
        compiler_params=pltpu.CompilerParams(
            dimension_semantics=("parallel", "parallel"), vmem_limit_bytes=VMEM_LIMIT),
        name="mix_ln1",
    )(attnT, convT, x, woa, woc, g, b)


def _extract_top(s, n_rows, steps, exact_ties, want_rank):
    T = s.shape[1]
    orig = s
    iota = lax.broadcasted_iota(I32, (n_rows, T), 0).astype(F32)
    rank = jnp.full((n_rows, T), float(steps), F32) if want_rank else None
    vals = []
    for a in range(steps):
        cur = jnp.max(s, axis=0, keepdims=True)
        hit = s == cur
        if exact_ties:
            first = jnp.min(jnp.where(hit, iota, float(n_rows)), axis=0, keepdims=True)
            hit = iota == first
        if want_rank:
            rank = jnp.where(hit, float(a), rank)
        s = jnp.where(hit, -jnp.inf, s)
        vals.append(cur)
    member = s != orig
    removed = jnp.sum(jnp.where(member, 1.0, 0.0), axis=0, keepdims=True)
    return rank, vals, member, removed


def _pack_words(x):
    bits = pltpu.bitcast(x.astype(BF16).astype(F32), I32)
    return bits | lax.shift_right_logical(bits, 16)


def _route_head(h, qT, s1_ref, s2_ref, r2_ref, c_ref, e1_ref, e2_ref, exact_ties):
    neg_inf = jnp.full((8, TR), -jnp.inf, F32)
    sub8 = lax.broadcasted_iota(I32, (8, TR), 0)
    q1 = qT[h * 128:h * 128 + PEER_HALF].astype(BF16)
    q2 = qT[h * 128 + PEER_HALF:(h + 1) * 128].astype(BF16)
    s1 = _dot(s1_ref[h], q1)
    s2 = _dot(s2_ref[h], q2)
    r1, v1, in1, n1 = _extract_top(s1, PEER_NKEYS, PEER_TOPK, exact_ties, want_rank=exact_ties)
    r2, v2, in2, n2 = _extract_top(s2, PEER_NKEYS, PEER_TOPK, exact_ties, want_rank=True)
    v1m = jnp.concatenate(v1, axis=0)
    v2m = jnp.concatenate(v2, axis=0)
    blocks = [v1[0] + v2m[:8], v1[0] + v2m[8:]]
    for a in range(1, 8):
        blocks.append(jnp.where(sub8 < PEER_TOPK // (a + 1), v1[a] + v2m[:8], neg_inf))
    blocks.append(v1m[8:] + v2[0])
    cand = jnp.concatenate(blocks, axis=0)
    _, _, chosen, n3 = _extract_top(cand, cand.shape[0], PEER_TOPK, exact_ties, want_rank=False)
    top = v1[0] + v2[0]
    z = jnp.sum(jnp.where(chosen, jnp.exp(cand - top), 0.0), axis=0, keepdims=True)
    chosen_f = chosen.astype(F32)
    per_a = [jnp.sum(chosen_f[:16], axis=0, keepdims=True)]
    for a in range(1, 8):
        per_a.append(jnp.sum(chosen_f[8 + 8 * a:16 + 8 * a], axis=0, keepdims=True))
    for a in range(8, PEER_TOPK):
        per_a.append(chosen_f[64 + a:65 + a])
    cnt = jnp.zeros((PEER_NKEYS, TR), F32)
    for a in range(PEER_TOPK):
        is_a = (r1 == float(a)) if exact_ties else (s1 == v1[a])
        cnt = jnp.where(is_a, per_a[a], cnt)
    r2_ref[h] = r2.astype(BF16)
    c_ref[h] = _pack_words(cnt)
    e1_ref[h] = _pack_words(jnp.where(in1, jnp.exp(s1 - v1[0]), 0.0) / z)
    e2_ref[h] = jnp.where(in2, jnp.exp(s2 - v2[0]), 0.0).astype(BF16)
    return jnp.maximum(jnp.maximum(n1, n2), n3)


def _route_kernel(x_ref, wq_ref, s1_ref, s2_ref, r2_ref, c_ref, e1_ref, e2_ref, q_scr):
    q_scr[...] = _dot(wq_ref[...], x_ref[...])
    refs = (s1_ref, s2_ref, r2_ref, c_ref, e1_ref, e2_ref)
    worst = [_route_head(h, q_scr, *refs, exact_ties=False) for h in range(PEER_HEADS)]
    for h in range(PEER_HEADS):
        @pl.when(jnp.max(worst[h]) > float(PEER_TOPK))
        def _():
            _route_head(h, q_scr, *refs, exact_ties=True)


def _route(x1b, wqT, sub1, sub2):
    _, N = x1b.shape
    full = lambda shape: pl.BlockSpec(shape, lambda i: (0,) * len(shape))
    spec = pl.BlockSpec((PEER_HEADS, PEER_NKEYS, TR), lambda i: (0, 0, i))
    i32 = jax.ShapeDtypeStruct((PEER_HEADS, PEER_NKEYS, N), I32)
    b16 = jax.ShapeDtypeStruct((PEER_HEADS, PEER_NKEYS, N), BF16)
    return pl.pallas_call(
        _route_kernel,
        grid=(N // TR,),
        in_specs=[pl.BlockSpec((D_MODEL, TR), lambda i: (0, i)),
                  full(wqT.shape), full(sub1.shape), full(sub2.shape)],
        out_specs=[spec, spec, spec, spec],
        out_shape=[b16, i32, i32, b16],
        scratch_shapes=[pltpu.VMEM((D_MODEL, TR), F32)],
        compiler_params=pltpu.CompilerParams(
            dimension_semantics=("parallel",), vmem_limit_bytes=VMEM_LIMIT),
        name="peer_route",
    )(x1b, wqT, sub1, sub2)


def _peer_kernel(xb_ref, x_ref, r2_ref, c_ref, e1_ref, e2_ref, u_ref, vt_ref, g_ref, b_ref,
                 o_ref, acc_ref, hh_ref, gh_ref):
    c = pl.program_id(1)

    @pl.when(c == 0)
    def _():
        acc_ref[...] = jnp.zeros_like(acc_ref)

    def packed_rows(words):
        return pltpu.bitcast(jnp.broadcast_to(words, (GROWS // 2, TB)), BF16)

    zero = jnp.zeros((), BF16)
    per_piece = PIECE // PEER_NKEYS

    def gates(k):
        for s in range(k * per_piece, (k + 1) * per_piece):
            for rb in range(PEER_NKEYS // GROWS):
                rs = slice(rb * GROWS, (rb + 1) * GROWS)
                g = None
                for h in range(PEER_HEADS):
                    cb = packed_rows(c_ref[h, s:s + 1, :])
                    eb = packed_rows(e1_ref[h, s:s + 1, :])
                    t = jnp.where(r2_ref[h, rs, :] < cb, e2_ref[h, rs, :], zero) * eb
                    g = t if g is None else g + t
                gh_ref[s * PEER_NKEYS + rb * GROWS:s * PEER_NKEYS + (rb + 1) * GROWS, :] = g

    def up(k):
        ps = slice(k * PIECE, (k + 1) * PIECE)
        hh_ref[ps, :] = _dot(u_ref[ps, :], xb_ref[...])

    def activate(k):
        ps = slice(k * PIECE, (k + 1) * PIECE)
        hh = hh_ref[ps, :]
        act = (hh * (1.0 + lax.erf(hh * (2.0 ** 0.5)))).astype(BF16)
        gh_ref[ps, :] = gh_ref[ps, :] * act

    def down(k0, k1):
        ps = slice(k0 * PIECE, k1 * PIECE)
        acc_ref[...] += _dot(vt_ref[:, ps], gh_ref[ps, :])

    n_piece = EC // PIECE
    down_every = max(n_piece // 2, 1)
    up(0)
    for k in range(n_piece):
        if k + 1 < n_piece:
            up(k + 1)
        gates(k)
        activate(k)
        if (k + 1) % down_every == 0:
            down(k + 1 - down_every, k + 1)

    @pl.when(c == pl.num_programs(1) - 1)
    def _():
        y = DEEPNORM_ALPHA * x_ref[...] + acc_ref[...]
        x2 = _ln_rows(y, _lane_tile(g_ref[...], TB), _lane_tile(b_ref[...], TB))
        o_ref[...] = x2.T


def _peer(x1b, x1, r2, cnt, e1, e2, u, vt, g, b):
    _, N = x1.shape
    rows = EC // PEER_NKEYS
    full = lambda shape: pl.BlockSpec(shape, lambda j, c: (0,) * len(shape))
    tok = pl.BlockSpec((PEER_HEADS, PEER_NKEYS, TB), lambda j, c: (0, 0, j))
    row = pl.BlockSpec((PEER_HEADS, rows, TB), lambda j, c: (0, c, j))
    return pl.pallas_call(
        _peer_kernel,
        grid=(N // TB, PEER_NEXPERTS // EC),
        in_specs=[
            pl.BlockSpec((D_MODEL, TB), lambda j, c: (0, j)),
            pl.BlockSpec((D_MODEL, TB), lambda j, c: (0, j)),
            tok, row, row, tok,
            pl.BlockSpec((EC, D_MODEL), lambda j, c: (c, 0)),
            pl.BlockSpec((D_MODEL, EC), lambda j, c: (0, c)),
            full(g.shape), full(b.shape),
        ],
        out_specs=pl.BlockSpec((TB, D_MODEL), lambda j, c: (j, 0)),
        out_shape=jax.ShapeDtypeStruct((N, D_MODEL), F32),
        scratch_shapes=[pltpu.VMEM((D_MODEL, TB), F32),
                        pltpu.VMEM((EC, TB), F32),
                        pltpu.VMEM((EC, TB), BF16)],
        compiler_params=pltpu.CompilerParams(
            dimension_semantics=("parallel", "arbitrary"), vmem_limit_bytes=VMEM_LIMIT),
        name="peer_dense",
    )(x1b, x1, r2, cnt, e1, e2, u, vt, g, b)


def _col_bcast(v, n):
    return jnp.broadcast_to(v.astype(F32)[:, None], (v.shape[0], n))


def kernel(x, w_in, idx_k_ln_g, idx_k_ln_b, conv_dw_w, conv_dw_b, conv_ln_g, conv_ln_b,
           w_out, ln1_g, ln1_b, peer_w_q, peer_sub_keys1, peer_sub_keys2, peer_u, peer_v,
           ln2_g, ln2_b):
    B, L, _ = x.shape
    assert L % TM == 0 and L % TQ == 0 and TQ == TK and (B * L) % TB == 0
    l = 0

    w = w_in[l]
    wT = w.T.astype(BF16)
    wm = jnp.concatenate([wT[OFF_Q:OFF_KI]], axis=0)
    ws = jnp.concatenate([wT[OFF_KI:OFF_CONV],
                          jnp.zeros((LANES - IDX_DIM - IDX_HEADS, D_MODEL), BF16)], axis=0)
    wc = w[:, OFF_CONV:].astype(BF16)
    woT = w_out[l].T.astype(BF16)
    woa, woc = woT[:, :D_ATTN], woT[:, D_ATTN:]
    wqT = peer_w_q[l].T.astype(BF16)
    sub1 = peer_sub_keys1[l].astype(BF16)
    sub2 = peer_sub_keys2[l].astype(BF16)
    u = (0.5 * peer_u[l]).astype(BF16)
    vt = peer_v[l].T.astype(BF16)

    pos = jnp.arange(L, dtype=F32)
    inv_freq = ROPE_THETA ** (-jnp.arange(0, HEAD_DIM, 2, dtype=F32) / HEAD_DIM)
    ang = inv_freq[:, None] * pos[None, :]
    cosT = jnp.concatenate([jnp.cos(ang), jnp.cos(ang)], axis=0)
    sinT = jnp.concatenate([jnp.sin(ang), jnp.sin(ang)], axis=0)

    qz, qiT, vTc, wiT, k, ki, h = _proj(
        x, wm, ws, wc, cosT, sinT, _col_bcast(idx_k_ln_g[l], TM), _col_bcast(idx_k_ln_b[l], TM))

    dw = jnp.concatenate([conv_dw_w[l], jnp.zeros((1, D_CONV), F32)], axis=0)
    convT = _conv(h, dw, conv_dw_b[l][None, :], conv_ln_g[l][None, :], conv_ln_b[l][None, :])

    attnT = _dsa(qz, qiT, wiT, k, ki, vTc)

    x1, x1b = _mix(attnT, convT, x, woa, woc,
                   _col_bcast(ln1_g[l], LANES), _col_bcast(ln1_b[l], LANES))

    r2, cnt, e1, e2 = _route(x1b, wqT, sub1, sub2)
    out = _peer(x1b, x1, r2, cnt, e1, e2, u, vt,
                _col_bcast(ln2_g[l], LANES), _col_bcast(ln2_b[l], LANES))
    return out.reshape(B, L, D_MODEL)
```

```python
import functools

import jax
import jax.numpy as jnp
from jax import lax
from jax.experimental import pallas as pl
from jax.experimental.pallas import tpu as pltpu

F32 = jnp.float32
BF16 = jnp.bfloat16
I32 = jnp.int32
I16 = jnp.int16

D_MODEL = 1024
ATTN_HEADS = 8
HEAD_DIM = 64
D_ATTN = ATTN_HEADS * HEAD_DIM
D_CONV = 512
CONV_WIDTH = 31
IDX_HEADS = 8
IDX_DIM = 64
TOPK_MAX = 256
ROPE_THETA = 10000.0
PEER_HEADS = 8
PEER_NKEYS = 128
PEER_NEXPERTS = PEER_NKEYS * PEER_NKEYS
PEER_HALF = 64
PEER_TOPK = 16
LN_EPS = 1e-5
DEPTH = 1
DEEPNORM_ALPHA = (2.0 * DEPTH) ** 0.25

OFF_Q = 0
OFF_K = OFF_Q + D_ATTN
OFF_V = OFF_K + D_ATTN
OFF_QI = OFF_V + D_ATTN
OFF_KI = OFF_QI + IDX_HEADS * IDX_DIM
OFF_WI = OFF_KI + IDX_DIM
OFF_CONV = OFF_WI + IDX_HEADS

LANES = 128
SUBLANES = 8
PACK = 16
VMEM_LIMIT = 56 * 1024 * 1024

TM = 512
TC = 512
HALO = 32
TQ = 256
TK = 256
VROWS = 80
LOOKAHEAD = 4
TN = 512
TR = 256
TB = 512
EC = 1024
PIECE = 256
GROWS = 128

LOG2E = 1.4426950408889634
Q_SCALE = HEAD_DIM ** -0.5 * LOG2E
NEG_BIG = -1e30
INT_MIN = -2 ** 31
I16_MIN = -2 ** 15
I16_MAX = 2 ** 15 - 1


def _nt_dot(a, b):
    return lax.dot_general(a, b, (((1,), (1,)), ((), ())), preferred_element_type=F32)


def _dot(a, b):
    return jnp.dot(a, b, preferred_element_type=F32)


def _lane_tile(a, n):
    reps = n // LANES
    return a if reps == 1 else jnp.concatenate([a] * reps, axis=1)


def _rope_rows(z, cos, sin):
    x1, x2 = z[:32], z[32:]
    top = x1 * cos[:32] - x2 * sin[:32]
    bot = x2 * cos[32:] + x1 * sin[32:]
    return top, bot


def _proj_kernel(x_ref, wm_ref, ws_ref, wc_ref, cos_ref, sin_ref, kig_ref, kib_ref,
                 qz_ref, qi_ref, v_ref, wi_ref, k_ref, ki_ref, h_ref, kt_scr):
    xb = x_ref[...].astype(BF16)
    cos = cos_ref[...]
    sin = sin_ref[...]
    pT = _nt_dot(wm_ref[...], xb)

    zeros64 = jnp.zeros((HEAD_DIM, TM), BF16)
    for h in range(ATTN_HEADS):
        top, bot = _rope_rows(pT[OFF_Q + h * 64:OFF_Q + (h + 1) * 64], cos, sin)
        qh = (jnp.concatenate([top, bot], axis=0) * Q_SCALE).astype(BF16)
        if h % 2 == 0:
            qz_ref[h * 128:h * 128 + 64, :] = qh
            qz_ref[h * 128 + 64:(h + 1) * 128, :] = zeros64
        else:
            qz_ref[h * 128:h * 128 + 64, :] = zeros64
            qz_ref[h * 128 + 64:(h + 1) * 128, :] = qh
        top, bot = _rope_rows(pT[OFF_K + h * 64:OFF_K + (h + 1) * 64], cos, sin)
        kt_scr[h * 64:h * 64 + 32, :] = top
        kt_scr[h * 64 + 32:(h + 1) * 64, :] = bot
        top, bot = _rope_rows(pT[OFF_QI + h * 64:OFF_QI + (h + 1) * 64], cos, sin)
        qi_ref[h * 64:h * 64 + 32, :] = top.astype(BF16)
        qi_ref[h * 64 + 32:(h + 1) * 64, :] = bot.astype(BF16)
    k_ref[...] = kt_scr[...].T.astype(BF16)

    vT = pT[OFF_V:OFF_V + D_ATTN].astype(BF16)
    pad_rows = VROWS - HEAD_DIM
    ones_row = (lax.broadcasted_iota(I32, (pad_rows, TK), 0) == 0).astype(F32).astype(BF16)
    for j in range(TM // TK):
        for h in range(ATTN_HEADS):
            v_ref[j, h * VROWS:h * VROWS + HEAD_DIM, :] = vT[h * 64:(h + 1) * 64, j * TK:(j + 1) * TK]
            v_ref[j, h * VROWS + HEAD_DIM:(h + 1) * VROWS, :] = ones_row

    sT = _nt_dot(ws_ref[...], xb)
    z = sT[:IDX_DIM]
    mu = jnp.mean(z, axis=0, keepdims=True)
    var = jnp.mean(jnp.square(z - mu), axis=0, keepdims=True)
    y = (z - mu) * lax.rsqrt(var + LN_EPS) * kig_ref[...] + kib_ref[...]
    top, bot = _rope_rows(y, cos, sin)
    kiT = jnp.concatenate([top, bot, jnp.zeros((LANES - IDX_DIM, TM), F32)], axis=0)
    ki_ref[...] = kiT.T[:, :IDX_DIM].astype(BF16)
    wi_ref[...] = sT[IDX_DIM:IDX_DIM + IDX_HEADS] * (IDX_HEADS ** -0.5)

    a = _dot(xb, wc_ref[...])
    h_ref[...] = a[:, :D_CONV] * jax.nn.sigmoid(a[:, D_CONV:])


def _proj(x, wm, ws, wc, cosT, sinT, kig, kib):
    B, L, _ = x.shape
    grid = (B, L // TM)
    full = lambda shape: pl.BlockSpec(shape, lambda b, i: (0,) * len(shape))
    return pl.pallas_call(
        _proj_kernel,
        grid=grid,
        in_specs=[
            pl.BlockSpec((None, TM, D_MODEL), lambda b, i: (b, i, 0)),
            full(wm.shape), full(ws.shape), full(wc.shape),
            pl.BlockSpec((HEAD_DIM, TM), lambda b, i: (0, i)),
            pl.BlockSpec((HEAD_DIM, TM), lambda b, i: (0, i)),
            full(kig.shape), full(kib.shape),
        ],
        out_specs=[
            pl.BlockSpec((None, ATTN_HEADS * 128, TM), lambda b, i: (b, 0, i)),
            pl.BlockSpec((None, D_ATTN, TM), lambda b, i: (b, 0, i)),
            pl.BlockSpec((None, TM // TK, ATTN_HEADS * VROWS, TK), lambda b, i: (b, i, 0, 0)),
            pl.BlockSpec((None, IDX_HEADS, TM), lambda b, i: (b, 0, i)),
            pl.BlockSpec((None, TM, D_ATTN), lambda b, i: (b, i, 0)),
            pl.BlockSpec((None, TM, IDX_DIM), lambda b, i: (b, i, 0)),
            pl.BlockSpec((None, TM, D_CONV), lambda b, i: (b, i, 0)),
        ],
        out_shape=[
            jax.ShapeDtypeStruct((B, ATTN_HEADS * 128, L), BF16),
            jax.ShapeDtypeStruct((B, D_ATTN, L), BF16),
            jax.ShapeDtypeStruct((B, L // TK, ATTN_HEADS * VROWS, TK), BF16),
            jax.ShapeDtypeStruct((B, IDX_HEADS, L), F32),
            jax.ShapeDtypeStruct((B, L, D_ATTN), BF16),
            jax.ShapeDtypeStruct((B, L, IDX_DIM), BF16),
            jax.ShapeDtypeStruct((B, L, D_CONV), F32),
        ],
        scratch_shapes=[pltpu.VMEM((D_ATTN, TM), F32)],
        compiler_params=pltpu.CompilerParams(
            dimension_semantics=("parallel", "parallel"), vmem_limit_bytes=VMEM_LIMIT),
        name="proj",
    )(x, wm, ws, wc, cosT, sinT, kig, kib)


def _conv_kernel(h_ref, halo_ref, w_ref, b_ref, g_ref, beta_ref, o_ref, buf, shifted, acc):
    i = pl.program_id(1)
    rows = HALO + TC
    buf[0:HALO, :] = jnp.where(i > 0, halo_ref[...], 0.0)
    buf[HALO:rows, :] = h_ref[...]
    buf[rows:rows + SUBLANES, :] = jnp.zeros((SUBLANES, D_CONV), F32)
    for r in range(SUBLANES):
        shifted[r] = buf[r:r + rows, :]
    base = HALO - (CONV_WIDTH - 1)
    for c in range(D_CONV // LANES):
        cs = slice(c * LANES, (c + 1) * LANES)
        for rb in range(TC // LANES):
            a = jnp.zeros((LANES, LANES), F32)
            for j in range(CONV_WIDTH):
                phase = (base + j) % SUBLANES
                s = base + j - phase + rb * LANES
                a = a + shifted[phase, s:s + LANES, cs] * w_ref[j:j + 1, cs]
            acc[rb * LANES:(rb + 1) * LANES, cs] = a
    y = acc[...] + b_ref[...]
    mu = jnp.mean(y, axis=-1, keepdims=True)
    var = jnp.mean(jnp.square(y - mu), axis=-1, keepdims=True)
    y = (y - mu) * lax.rsqrt(var + LN_EPS) * g_ref[...] + beta_ref[...]
    y = y * jax.nn.sigmoid(y)
    o_ref[...] = y.T.astype(BF16)


def _conv(h, w, b, g, beta):
    B, L, _ = h.shape
    full = lambda shape: pl.BlockSpec(shape, lambda b_, i: (0,) * len(shape))
    return pl.pallas_call(
        _conv_kernel,
        grid=(B, L // TC),
        in_specs=[
            pl.BlockSpec((None, TC, D_CONV), lambda b_, i: (b_, i, 0)),
            pl.BlockSpec((None, HALO, D_CONV),
                         lambda b_, i: (b_, jnp.maximum(i * (TC // HALO) - 1, 0), 0)),
            full(w.shape), full(b.shape), full(g.shape), full(beta.shape),
        ],
        out_specs=pl.BlockSpec((None, D_CONV, TC), lambda b_, i: (b_, 0, i)),
        out_shape=jax.ShapeDtypeStruct((B, D_CONV, L), BF16),
        scratch_shapes=[pltpu.VMEM((HALO + TC + SUBLANES, D_CONV), F32),
                        pltpu.VMEM((SUBLANES, HALO + TC, D_CONV), F32),
                        pltpu.VMEM((TC, D_CONV), F32)],
        compiler_params=pltpu.CompilerParams(
            dimension_semantics=("parallel", "parallel"), vmem_limit_bytes=VMEM_LIMIT),
        name="conv",
    )(h, h, w, b, g, beta)


def _dsa_kernel(n_sel, qz_ref, qi_ref, wi_ref, k_ref, ki_ref, v_ref, ltri_ref, o_ref,
                keys_ref, hi_ref, lo_ref, acc_ref, m_ref, bias_ref, att_ref, ties_ref):
    qb = pl.program_id(1)
    nc = qb + 1
    rows = lax.broadcasted_iota(I32, (TK, TQ), 0)
    cols = lax.broadcasted_iota(I32, (TK, TQ), 1)

    def score_chunk(c, carry):
        kic = ki_ref[pl.ds(pl.multiple_of(c * TK, TK), TK), :]
        s = jnp.zeros((TK, TQ), F32)
        for h in range(IDX_HEADS):
            z = _dot(kic, qi_ref[h * IDX_DIM:(h + 1) * IDX_DIM, :])
            s = s + wi_ref[h:h + 1, :] * jnp.maximum(z, 0.0)
        s = jnp.where(s == 0.0, 0.0, s)
        bits = pltpu.bitcast(s, I32)
        key = jnp.where(bits < 0, bits ^ jnp.int32(0x7FFFFFFF), bits)
        valid = (c * TK + rows) <= (qb * TQ + cols)
        key = jnp.where(valid, key, jnp.int32(INT_MIN))
        keys_ref[c] = key
        hi_ref[c] = lax.shift_right_arithmetic(key, 16).astype(I16)
        return carry

    lax.fori_loop(0, nc, score_chunk, 0)

    n_pair = (nc + 1) // 2

    @pl.when(nc % 2 == 1)
    def _():
        keys_ref[nc] = jnp.full((TK, TQ), INT_MIN, I32)
        hi_ref[nc] = jnp.full((TK, TQ), I16_MIN, I16)

    def count_ge(cand):
        def body(i, cnt):
            a = jnp.sum((keys_ref[2 * i] >= cand).astype(I32), axis=0, keepdims=True)
            b = jnp.sum((keys_ref[2 * i + 1] >= cand).astype(I32), axis=0, keepdims=True)
            return cnt + (a + b)
        return lax.fori_loop(0, n_pair, body, jnp.zeros((1, TQ), I32))

    def count_ge16(ref, cand):
        c16 = cand.astype(I16)
        one, none = jnp.ones((), I16), jnp.zeros((), I16)

        def col_count(x):
            hits = jnp.where(x >= c16, one, none)
            part = hits[0:PACK]
            for r in range(1, TK // PACK):
                part = part + hits[r * PACK:(r + 1) * PACK]
            return jnp.sum(part.astype(I32), axis=0, keepdims=True)

        def body(i, cnt):
            return cnt + (col_count(ref[2 * i]) + col_count(ref[2 * i + 1]))
        return lax.fori_loop(0, n_pair, body, jnp.zeros((1, TQ), I32))

    def search16(ref, need):
        def bit_step(i, prefix):
            cand = prefix + lax.shift_left(jnp.int32(1), 15 - i)
            return jnp.where(count_ge16(ref, cand) >= need, cand, prefix)
        return lax.fori_loop(0, 16, bit_step, jnp.full((1, TQ), I16_MIN, I32))

    t_hi = search16(hi_ref, n_sel)
    above = jnp.where(t_hi == I16_MAX, 0, count_ge16(hi_ref, jnp.minimum(t_hi + 1, I16_MAX)))

    def low_chunk(c, carry):
        key = keys_ref[c]
        low = (key & jnp.int32(0xFFFF)) + jnp.int32(I16_MIN)
        in_bucket = lax.shift_right_arithmetic(key, 16) == t_hi
        lo_ref[c] = jnp.where(in_bucket, low, jnp.int32(I16_MIN)).astype(I16)
        return carry

    lax.fori_loop(0, 2 * n_pair, low_chunk, 0)
    t_lo = search16(lo_ref, n_sel - above)
    prefix = lax.shift_left(t_hi, 16) + (t_lo - jnp.int32(I16_MIN))
    thr = jnp.maximum(prefix, jnp.int32(INT_MIN + 1))
    take = (n_sel - count_ge(thr + 1)).astype(F32)

    acc_ref[...] = jnp.zeros_like(acc_ref)
    m_ref[...] = jnp.full_like(m_ref, NEG_BIG)
    ties_ref[...] = jnp.zeros_like(ties_ref)

    def attn_chunk(c, carry):
        kk = keys_ref[c]
        tie = kk == thr
        tie_b = jnp.where(tie, 1.0, 0.0).astype(BF16)
        before = _dot(ltri_ref[...], tie_b) + ties_ref[...]
        tie_bias = jnp.where(before < take, 0.0, NEG_BIG)
        bias_ref[...] = jnp.where(kk > thr, 0.0, jnp.where(tie, tie_bias, NEG_BIG))
        ties_ref[...] += jnp.sum(tie_b.astype(F32), axis=0, keepdims=True)
        start = pl.multiple_of(c * TK, TK)

        def logits(h):
            kc = k_ref[pl.ds(start, TK), (h // 2) * 128:(h // 2 + 1) * 128]
            att_ref[h] = _dot(kc, qz_ref[h * 128:(h + 1) * 128, :]) + bias_ref[...]

        for h in range(min(LOOKAHEAD, ATTN_HEADS)):
            logits(h)
        for h in range(ATTN_HEADS):
            if h % 2 == 0:
                for hh in range(h + LOOKAHEAD, min(h + LOOKAHEAD + 2, ATTN_HEADS)):
                    logits(hh)
            m_old = m_ref[h:h + 1, :]
            m_new = jnp.maximum(m_old, jnp.max(att_ref[h], axis=0, keepdims=True))
            p = jnp.exp2(att_ref[h] - m_new)
            alpha = jnp.exp2(m_old - m_new)
            hs = slice(h * VROWS, (h + 1) * VROWS)
            pv = _dot(v_ref[c, hs, :], p.astype(BF16))
            acc_ref[hs, :] = alpha * acc_ref[hs, :] + pv
            m_ref[h:h + 1, :] = m_new
        return carry

    lax.fori_loop(0, nc, attn_chunk, 0)

    for h in range(ATTN_HEADS):
        num = acc_ref[h * VROWS:h * VROWS + HEAD_DIM, :]
        den = acc_ref[h * VROWS + HEAD_DIM:h * VROWS + HEAD_DIM + 1, :]
        o_ref[h * 64:(h + 1) * 64, :] = (num / den).astype(BF16)


def _dsa(qz, qiT, wiT, k, ki, vTc):
    B, _, L = qiT.shape
    n_sel = min(TOPK_MAX, L // 4)
    nck = L // TK
    ltri = jnp.tril(jnp.ones((TK, TK), BF16), -1)
    return pl.pallas_call(
        functools.partial(_dsa_kernel, n_sel),
        grid=(B, L // TQ),
        in_specs=[
            pl.BlockSpec((None, ATTN_HEADS * 128, TQ), lambda b, q: (b, 0, q)),
            pl.BlockSpec((None, D_ATTN, TQ), lambda b, q: (b, 0, q)),
            pl.BlockSpec((None, IDX_HEADS, TQ), lambda b, q: (b, 0, q)),
            pl.BlockSpec((None, L, D_ATTN), lambda b, q: (b, 0, 0)),
            pl.BlockSpec((None, L, IDX_DIM), lambda b, q: (b, 0, 0)),
            pl.BlockSpec((None, nck, ATTN_HEADS * VROWS, TK), lambda b, q: (b, 0, 0, 0)),
            pl.BlockSpec((TK, TK), lambda b, q: (0, 0)),
        ],
        out_specs=pl.BlockSpec((None, D_ATTN, TQ), lambda b, q: (b, 0, q)),
        out_shape=jax.ShapeDtypeStruct((B, D_ATTN, L), BF16),
        scratch_shapes=[
            pltpu.VMEM((nck, TK, TQ), I32),
            pltpu.VMEM((nck, TK, TQ), I16),
            pltpu.VMEM((nck, TK, TQ), I16),
            pltpu.VMEM((ATTN_HEADS * VROWS, TQ), F32),
            pltpu.VMEM((ATTN_HEADS, TQ), F32),
            pltpu.VMEM((TK, TQ), F32),
            pltpu.VMEM((ATTN_HEADS, TK, TQ), F32),
            pltpu.VMEM((1, TQ), F32),
        ],
        compiler_params=pltpu.CompilerParams(
            dimension_semantics=("parallel", "arbitrary"), vmem_limit_bytes=VMEM_LIMIT),
        name="dsa",
    )(qz, qiT, wiT, k, ki, vTc, ltri)


def _ln_rows(y, g, b):
    mu = jnp.mean(y, axis=0, keepdims=True)
    var = jnp.mean(jnp.square(y - mu), axis=0, keepdims=True)
    return (y - mu) * lax.rsqrt(var + LN_EPS) * g + b


def _mix_kernel(a_ref, c_ref, x_ref, wa_ref, wc_ref, g_ref, b_ref, o_ref, ob_ref):
    mix = _dot(wa_ref[...], a_ref[...]) + _dot(wc_ref[...], c_ref[...])
    y = DEEPNORM_ALPHA * x_ref[...].T + mix
    x1 = _ln_rows(y, _lane_tile(g_ref[...], TN), _lane_tile(b_ref[...], TN))
    o_ref[...] = x1
    ob_ref[...] = x1.astype(BF16)


def _mix(attnT, convT, x, woa, woc, g, b):
    B, L, _ = x.shape
    nl = L // TN
    full = lambda shape: pl.BlockSpec(shape, lambda b_, i: (0,) * len(shape))
    return pl.pallas_call(
        _mix_kernel,
        grid=(B, nl),
        in_specs=[
            pl.BlockSpec((None, D_ATTN, TN), lambda b_, i: (b_, 0, i)),
            pl.BlockSpec((None, D_CONV, TN), lambda b_, i: (b_, 0, i)),
            pl.BlockSpec((None, TN, D_MODEL), lambda b_, i: (b_, i, 0)),
            full(woa.shape), full(woc.shape), full(g.shape), full(b.shape),
        ],
        out_specs=[
            pl.BlockSpec((D_MODEL, TN), lambda b_, i: (0, b_ * nl + i)),
            pl.BlockSpec((D_MODEL, TN), lambda b_, i: (0, b_ * nl + i)),
        ],
        out_shape=[
            jax.ShapeDtypeStruct((D_MODEL, B * L), F32),
            jax.ShapeDtypeStruct((D_MODEL, B * L), BF16),
        ],
        compiler_params=pltpu.CompilerParams(
            dimension_semantics=("parallel", "parallel"), vmem_limit_bytes=VMEM_LIMIT),
        name="mix_ln1",
    )(attnT, convT, x, woa, woc, g, b)


def _extract_top(s, n_rows, steps, exact_ties, want_rank):
    T = s.shape[1]
    orig = s
    iota = lax.broadcasted_iota(I32, (n_rows, T), 0).astype(F32)
    rank = jnp.full((n_rows, T), float(steps), F32) if want_rank else None
    vals = []
    for a in range(steps):
        cur = jnp.max(s, axis=0, keepdims=True)
        hit = s == cur
        if exact_ties:
            first = jnp.min(jnp.where(hit, iota, float(n_rows)), axis=0, keepdims=True)
            hit = iota == first
        if want_rank:
            rank = jnp.where(hit, float(a), rank)
        s = jnp.where(hit, -jnp.inf, s)
        vals.append(cur)
    member = s != orig
    removed = jnp.sum(jnp.where(member, 1.0, 0.0), axis=0, keepdims=True)
    return rank, vals, member, removed


def _pack_words(x):
    bits = pltpu.bitcast(x.astype(BF16).astype(F32), I32)
    return bits | lax.shift_right_logical(bits, 16)


def _route_head(h, qT, s1_ref, s2_ref, r2_ref, c_ref, e1_ref, e2_ref, exact_ties):
    neg_inf = jnp.full((8, TR), -jnp.inf, F32)
    sub8 = lax.broadcasted_iota(I32, (8, TR), 0)
    q1 = qT[h * 128:h * 128 + PEER_HALF].astype(BF16)
    q2 = qT[h * 128 + PEER_HALF:(h + 1) * 128].astype(BF16)
    s1 = _dot(s1_ref[h], q1)
    s2 = _dot(s2_ref[h], q2)
    r1, v1, in1, n1 = _extract_top(s1, PEER_NKEYS, PEER_TOPK, exact_ties, want_rank=exact_ties)
    r2, v2, in2, n2 = _extract_top(s2, PEER_NKEYS, PEER_TOPK, exact_ties, want_rank=True)
    v1m = jnp.concatenate(v1, axis=0)
    v2m = jnp.concatenate(v2, axis=0)
    blocks = [v1[0] + v2m[:8], v1[0] + v2m[8:]]
    for a in range(1, 8):
        blocks.append(jnp.where(sub8 < PEER_TOPK // (a + 1), v1[a] + v2m[:8], neg_inf))
    blocks.append(v1m[8:] + v2[0])
    cand = jnp.concatenate(blocks, axis=0)
    _, _, chosen, n3 = _extract_top(cand, cand.shape[0], PEER_TOPK, exact_ties, want_rank=False)
    top = v1[0] + v2[0]
    z = jnp.sum(jnp.where(chosen, jnp.exp(cand - top), 0.0), axis=0, keepdims=True)
    chosen_f = chosen.astype(F32)
    per_a = [jnp.sum(chosen_f[:16], axis=0, keepdims=True)]
    for a in range(1, 8):
        per_a.append(jnp.sum(chosen_f[8 + 8 * a:16 + 8 * a], axis=0, keepdims=True))
    for a in range(8, PEER_TOPK):
        per_a.append(chosen_f[64 + a:65 + a])
    cnt = jnp.zeros((PEER_NKEYS, TR), F32)
    for a in range(PEER_TOPK):
        is_a = (r1 == float(a)) if exact_ties else (s1 == v1[a])
        cnt = jnp.where(is_a, per_a[a], cnt)
    r2_ref[h] = r2.astype(BF16)
    c_ref[h] = _pack_words(cnt)
    e1_ref[h] = _pack_words(jnp.where(in1, jnp.exp(s1 - v1[0]), 0.0) / z)
    e2_ref[h] = jnp.where(in2, jnp.exp(s2 - v2[0]), 0.0).astype(BF16)
    return jnp.maximum(jnp.maximum(n1, n2), n3)


def _route_kernel(x_ref, wq_ref, s1_ref, s2_ref, r2_ref, c_ref, e1_ref, e2_ref, q_scr):
    q_scr[...] = _dot(wq_ref[...], x_ref[...])
    refs = (s1_ref, s2_ref, r2_ref, c_ref, e1_ref, e2_ref)
    worst = [_route_head(h, q_scr, *refs, exact_ties=False) for h in range(PEER_HEADS)]
    for h in range(PEER_HEADS):
        @pl.when(jnp.max(worst[h]) > float(PEER_TOPK))
        def _():
            _route_head(h, q_scr, *refs, exact_ties=True)


def _route(x1b, wqT, sub1, sub2):
    _, N = x1b.shape
    full = lambda shape: pl.BlockSpec(shape, lambda i: (0,) * len(shape))
    spec = pl.BlockSpec((PEER_HEADS, PEER_NKEYS, TR), lambda i: (0, 0, i))
    i32 = jax.ShapeDtypeStruct((PEER_HEADS, PEER_NKEYS, N), I32)
    b16 = jax.ShapeDtypeStruct((PEER_HEADS, PEER_NKEYS, N), BF16)
    return pl.pallas_call(
        _route_kernel,
        grid=(N // TR,),
        in_specs=[pl.BlockSpec((D_MODEL, TR), lambda i: (0, i)),
                  full(wqT.shape), full(sub1.shape), full(sub2.shape)],
        out_specs=[spec, spec, spec, spec],
        out_shape=[b16, i32, i32, b16],
        scratch_shapes=[pltpu.VMEM((D_MODEL, TR), F32)],
        compiler_params=pltpu.CompilerParams(
            dimension_semantics=("parallel",), vmem_limit_bytes=VMEM_LIMIT),
        name="peer_route",
    )(x1b, wqT, sub1, sub2)


def _peer_kernel(xb_ref, x_ref, r2_ref, c_ref, e1_ref, e2_ref, u_ref, vt_ref, g_ref, b_ref,
                 o_ref, acc_ref, hh_ref, gh_ref):
    c = pl.program_id(1)

    @pl.when(c == 0)
    def _():
        acc_ref[...] = jnp.zeros_like(acc_ref)

    def packed_rows(words):
        return pltpu.bitcast(jnp.broadcast_to(words, (GROWS // 2, TB)), BF16)

    zero = jnp.zeros((), BF16)
    per_piece = PIECE // PEER_NKEYS

    def gates(k):
        for s in range(k * per_piece, (k + 1) * per_piece):
            for rb in range(PEER_NKEYS // GROWS):
                rs = slice(rb * GROWS, (rb + 1) * GROWS)
                g = None
                for h in range(PEER_HEADS):
                    cb = packed_rows(c_ref[h, s:s + 1, :])
                    eb = packed_rows(e1_ref[h, s:s + 1, :])
                    t = jnp.where(r2_ref[h, rs, :] < cb, e2_ref[h, rs, :], zero) * eb
                    g = t if g is None else g + t
                gh_ref[s * PEER_NKEYS + rb * GROWS:s * PEER_NKEYS + (rb + 1) * GROWS, :] = g

    def up(k):
        ps = slice(k * PIECE, (k + 1) * PIECE)
        hh_ref[ps, :] = _dot(u_ref[ps, :], xb_ref[...])

    def activate(k):
        ps = slice(k * PIECE, (k + 1) * PIECE)
        hh = hh_ref[ps, :]
        act = (hh * (1.0 + lax.erf(hh * (2.0 ** 0.5)))).astype(BF16)
        gh_ref[ps, :] = gh_ref[ps, :] * act

    def down(k0, k1):
        ps = slice(k0 * PIECE, k1 * PIECE)
        acc_ref[...] += _dot(vt_ref[:, ps], gh_ref[ps, :])

    n_piece = EC // PIECE
    down_every = max(n_piece // 2, 1)
    up(0)
    for k in range(n_piece):
        if k + 1 < n_piece:
            up(k + 1)
        gates(k)
        activate(k)
        if (k + 1) % down_every == 0:
            down(k + 1 - down_every, k + 1)

    @pl.when(c == pl.num_programs(1) - 1)
    def _():
        y = DEEPNORM_ALPHA * x_ref[...] + acc_ref[...]
        x2 = _ln_rows(y, _lane_tile(g_ref[...], TB), _lane_tile(b_ref[...], TB))
        o_ref[...] = x2.T


def _peer(x1b, x1, r2, cnt, e1, e2, u, vt, g, b):
    _, N = x1.shape
    rows = EC // PEER_NKEYS
    full = lambda shape: pl.BlockSpec(shape, lambda j, c: (0,) * len(shape))
    tok = pl.BlockSpec((PEER_HEADS, PEER_NKEYS, TB), lambda j, c: (0, 0, j))
    row = pl.BlockSpec((PEER_HEADS, rows, TB), lambda j, c: (0, c, j))
    return pl.pallas_call(
        _peer_kernel,
        grid=(N // TB, PEER_NEXPERTS // EC),
        in_specs=[
            pl.BlockSpec((D_MODEL, TB), lambda j, c: (0, j)),
            pl.BlockSpec((D_MODEL, TB), lambda j, c: (0, j)),
            tok, row, row, tok,
            pl.BlockSpec((EC, D_MODEL), lambda j, c: (c, 0)),
            pl.BlockSpec((D_MODEL, EC), lambda j, c: (0, c)),
            full(g.shape), full(b.shape),
        ],
        out_specs=pl.BlockSpec((TB, D_MODEL), lambda j, c: (j, 0)),
        out_shape=jax.ShapeDtypeStruct((N, D_MODEL), F32),
        scratch_shapes=[pltpu.VMEM((D_MODEL, TB), F32),
                        pltpu.VMEM((EC, TB), F32),
                        pltpu.VMEM((EC, TB), BF16)],
        compiler_params=pltpu.CompilerParams(
            dimension_semantics=("parallel", "arbitrary"), vmem_limit_bytes=VMEM_LIMIT),
        name="peer_dense",
    )(x1b, x1, r2, cnt, e1, e2, u, vt, g, b)


def _col_bcast(v, n):
    return jnp.broadcast_to(v.astype(F32)[:, None], (v.shape[0], n))


def kernel(x, w_in, idx_k_ln_g, idx_k_ln_b, conv_dw_w, conv_dw_b, conv_ln_g, conv_ln_b,
           w_out, ln1_g, ln1_b, peer_w_q, peer_sub_keys1, peer_sub_keys2, peer_u, peer_v,
           ln2_g, ln2_b):
    B, L, _ = x.shape
    assert L % TM == 0 and L % TQ == 0 and TQ == TK and (B * L) % TB == 0
    l = 0

    w = w_in[l]
    wT = w.T.astype(BF16)
    wm = jnp.concatenate([wT[OFF_Q:OFF_KI]], axis=0)
    ws = jnp.concatenate([wT[OFF_KI:OFF_CONV],
                          jnp.zeros((LANES - IDX_DIM - IDX_HEADS, D_MODEL), BF16)], axis=0)
    wc = w[:, OFF_CONV:].astype(BF16)
    woT = w_out[l].T.astype(BF16)
    woa, woc = woT[:, :D_ATTN], woT[:, D_ATTN:]
    wqT = peer_w_q[l].T.astype(BF16)
    sub1 = peer_sub_keys1[l].astype(BF16)
    sub2 = peer_sub_keys2[l].astype(BF16)
    u = (0.5 * peer_u[l]).astype(BF16)
    vt = peer_v[l].T.astype(BF16)

    pos = jnp.arange(L, dtype=F32)
    inv_freq = ROPE_THETA ** (-jnp.arange(0, HEAD_DIM, 2, dtype=F32) / HEAD_DIM)
    ang = inv_freq[:, None] * pos[None, :]
    cosT = jnp.concatenate([jnp.cos(ang), jnp.cos(ang)], axis=0)
    sinT = jnp.concatenate([jnp.sin(ang), jnp.sin(ang)], axis=0)

    qz, qiT, vTc, wiT, k, ki, h = _proj(
        x, wm, ws, wc, cosT, sinT, _col_bcast(idx_k_ln_g[l], TM), _col_bcast(idx_k_ln_b[l], TM))

    dw = jnp.concatenate([conv_dw_w[l], jnp.zeros((1, D_CONV), F32)], axis=0)
    convT = _conv(h, dw, conv_dw_b[l][None, :], conv_ln_g[l][None, :], conv_ln_b[l][None, :])

    attnT = _dsa(qz, qiT, wiT, k, ki, vTc)

    x1, x1b = _mix(attnT, convT, x, woa, woc,
                   _col_bcast(ln1_g[l], LANES), _col_bcast(ln1_b[l], LANES))

    r2, cnt, e1, e2 = _route(x1b, wqT, sub1, sub2)
    out = _peer(x1b, x1, r2, cnt, e1, e2, u, vt,
                _col_bcast(ln2_g[l], LANES), _col_bcast(ln2_b[l], LANES))
    return out.reshape(B, L, D_MODEL)
```

```python
import functools

import jax
import jax.numpy as jnp
from jax import lax
from jax.experimental import pallas as pl
from jax.experimental.pallas import tpu as pltpu

F32 = jnp.float32
BF16 = jnp.bfloat16
I32 = jnp.int32
I16 = jnp.int16

D_MODEL = 1024
ATTN_HEADS = 8
HEAD_DIM = 64
D_ATTN = ATTN_HEADS * HEAD_DIM
D_CONV = 512
CONV_WIDTH = 31
IDX_HEADS = 8
IDX_DIM = 64
TOPK_MAX = 256
ROPE_THETA = 10000.0
PEER_HEADS = 8
PEER_NKEYS = 128
PEER_NEXPERTS = PEER_NKEYS * PEER_NKEYS
PEER_HALF = 64
PEER_TOPK = 16
LN_EPS = 1e-5
DEPTH = 1
DEEPNORM_ALPHA = (2.0 * DEPTH) ** 0.25

OFF_Q = 0
OFF_K = OFF_Q + D_ATTN
OFF_V = OFF_K + D_ATTN
OFF_QI = OFF_V + D_ATTN
OFF_KI = OFF_QI + IDX_HEADS * IDX_DIM
OFF_WI = OFF_KI + IDX_DIM
OFF_CONV = OFF_WI + IDX_HEADS

LANES = 128
SUBLANES = 8
PACK = 16
VMEM_LIMIT = 56 * 1024 * 1024

TM = 512
TC = 512
HALO = 32
TQ = 256
TK = 256
VROWS = 80
LOOKAHEAD = 4
TN = 512
TR = 256
TB = 512
EC = 1024
PIECE = 256
GROWS = 128

LOG2E = 1.4426950408889634
Q_SCALE = HEAD_DIM ** -0.5 * LOG2E
NEG_BIG = -1e30
INT_MIN = -2 ** 31
I16_MIN = -2 ** 15
I16_MAX = 2 ** 15 - 1


def _nt_dot(a, b):
    return lax.dot_general(a, b, (((1,), (1,)), ((), ())), preferred_element_type=F32)


def _dot(a, b):
    return jnp.dot(a, b, preferred_element_type=F32)


def _lane_tile(a, n):
    reps = n // LANES
    return a if reps == 1 else jnp.concatenate([a] * reps, axis=1)


def _rope_rows(z, cos, sin):
    x1, x2 = z[:32], z[32:]
    top = x1 * cos[:32] - x2 * sin[:32]
    bot = x2 * cos[32:] + x1 * sin[32:]
    return top, bot


def _proj_kernel(x_ref, wm_ref, ws_ref, wc_ref, cos_ref, sin_ref, kig_ref, kib_ref,
                 qz_ref, qi_ref, v_ref, wi_ref, k_ref, ki_ref, h_ref, kt_scr):
    xb = x_ref[...].astype(BF16)
    cos = cos_ref[...]
    sin = sin_ref[...]
    pT = _nt_dot(wm_ref[...], xb)

    zeros64 = jnp.zeros((HEAD_DIM, TM), BF16)
    for h in range(ATTN_HEADS):
        top, bot = _rope_rows(pT[OFF_Q + h * 64:OFF_Q + (h + 1) * 64], cos, sin)
        qh = (jnp.concatenate([top, bot], axis=0) * Q_SCALE).astype(BF16)
        if h % 2 == 0:
            qz_ref[h * 128:h * 128 + 64, :] = qh
            qz_ref[h * 128 + 64:(h + 1) * 128, :] = zeros64
        else:
            qz_ref[h * 128:h * 128 + 64, :] = zeros64
            qz_ref[h * 128 + 64:(h + 1) * 128, :] = qh
        top, bot = _rope_rows(pT[OFF_K + h * 64:OFF_K + (h + 1) * 64], cos, sin)
        kt_scr[h * 64:h * 64 + 32, :] = top
        kt_scr[h * 64 + 32:(h + 1) * 64, :] = bot
        top, bot = _rope_rows(pT[OFF_QI + h * 64:OFF_QI + (h + 1) * 64], cos, sin)
        qi_ref[h * 64:h * 64 + 32, :] = top.astype(BF16)
        qi_ref[h * 64 + 32:(h + 1) * 64, :] = bot.astype(BF16)
    k_ref[...] = kt_scr[...].T.astype(BF16)

    vT = pT[OFF_V:OFF_V + D_ATTN].astype(BF16)
    pad_rows = VROWS - HEAD_DIM
    ones_row = (lax.broadcasted_iota(I32, (pad_rows, TK), 0) == 0).astype(F32).astype(BF16)
    for j in range(TM // TK):
        for h in range(ATTN_HEADS):
            v_ref[j, h * VROWS:h * VROWS + HEAD_DIM, :] = vT[h * 64:(h + 1) * 64, j * TK:(j + 1) * TK]
            v_ref[j, h * VROWS + HEAD_DIM:(h + 1) * VROWS, :] = ones_row

    sT = _nt_dot(ws_ref[...], xb)
    z = sT[:IDX_DIM]
    mu = jnp.mean(z, axis=0, keepdims=True)
    var = jnp.mean(jnp.square(z - mu), axis=0, keepdims=True)
    y = (z - mu) * lax.rsqrt(var + LN_EPS) * kig_ref[...] + kib_ref[...]
    top, bot = _rope_rows(y, cos, sin)
    kiT = jnp.concatenate([top, bot, jnp.zeros((LANES - IDX_DIM, TM), F32)], axis=0)
    ki_ref[...] = kiT.T[:, :IDX_DIM].astype(BF16)
    wi_ref[...] = sT[IDX_DIM:IDX_DIM + IDX_HEADS] * (IDX_HEADS ** -0.5)

    a = _dot(xb, wc_ref[...])
    h_ref[...] = a[:, :D_CONV] * jax.nn.sigmoid(a[:, D_CONV:])


def _proj(x, wm, ws, wc, cosT, sinT, kig, kib):
    B, L, _ = x.shape
    grid = (B, L // TM)
    full = lambda shape: pl.BlockSpec(shape, lambda b, i: (0,) * len(shape))
    return pl.pallas_call(
        _proj_kernel,
        grid=grid,
        in_specs=[
            pl.BlockSpec((None, TM, D_MODEL), lambda b, i: (b, i, 0)),
            full(wm.shape), full(ws.shape), full(wc.shape),
            pl.BlockSpec((HEAD_DIM, TM), lambda b, i: (0, i)),
            pl.BlockSpec((HEAD_DIM, TM), lambda b, i: (0, i)),
            full(kig.shape), full(kib.shape),
        ],
        out_specs=[
            pl.BlockSpec((None, ATTN_HEADS * 128, TM), lambda b, i: (b, 0, i)),
            pl.BlockSpec((None, D_ATTN, TM), lambda b, i: (b, 0, i)),
            pl.BlockSpec((None, TM // TK, ATTN_HEADS * VROWS, TK), lambda b, i: (b, i, 0, 0)),
            pl.BlockSpec((None, IDX_HEADS, TM), lambda b, i: (b, 0, i)),
            pl.BlockSpec((None, TM, D_ATTN), lambda b, i: (b, i, 0)),
            pl.BlockSpec((None, TM, IDX_DIM), lambda b, i: (b, i, 0)),
            pl.BlockSpec((None, TM, D_CONV), lambda b, i: (b, i, 0)),
        ],
        out_shape=[
            jax.ShapeDtypeStruct((B, ATTN_HEADS * 128, L), BF16),
            jax.ShapeDtypeStruct((B, D_ATTN, L), BF16),
            jax.ShapeDtypeStruct((B, L // TK, ATTN_HEADS * VROWS, TK), BF16),
            jax.ShapeDtypeStruct((B, IDX_HEADS, L), F32),
            jax.ShapeDtypeStruct((B, L, D_ATTN), BF16),
            jax.ShapeDtypeStruct((B, L, IDX_DIM), BF16),
            jax.ShapeDtypeStruct((B, L, D_CONV), F32),
        ],
        scratch_shapes=[pltpu.VMEM((D_ATTN, TM), F32)],
        compiler_params=pltpu.CompilerParams(
            dimension_semantics=("parallel", "parallel"), vmem_limit_bytes=VMEM_LIMIT),
        name="proj",
    )(x, wm, ws, wc, cosT, sinT, kig, kib)


def _conv_kernel(h_ref, halo_ref, w_ref, b_ref, g_ref, beta_ref, o_ref, buf, shifted, acc):
    i = pl.program_id(1)
    rows = HALO + TC
    buf[0:HALO, :] = jnp.where(i > 0, halo_ref[...], 0.0)
    buf[HALO:rows, :] = h_ref[...]
    buf[rows:rows + SUBLANES, :] = jnp.zeros((SUBLANES, D_CONV), F32)
    for r in range(SUBLANES):
        shifted[r] = buf[r:r + rows, :]
    base = HALO - (CONV_WIDTH - 1)
    for c in range(D_CONV // LANES):
        cs = slice(c * LANES, (c + 1) * LANES)
        for rb in range(TC // LANES):
            a = jnp.zeros((LANES, LANES), F32)
            for j in range(CONV_WIDTH):
                phase = (base + j) % SUBLANES
                s = base + j - phase + rb * LANES
                a = a + shifted[phase, s:s + LANES, cs] * w_ref[j:j + 1, cs]
            acc[rb * LANES:(rb + 1) * LANES, cs] = a
    y = acc[...] + b_ref[...]
    mu = jnp.mean(y, axis=-1, keepdims=True)
    var = jnp.mean(jnp.square(y - mu), axis=-1, keepdims=True)
    y = (y - mu) * lax.rsqrt(var + LN_EPS) * g_ref[...] + beta_ref[...]
    y = y * jax.nn.sigmoid(y)
    o_ref[...] = y.T.astype(BF16)


def _conv(h, w, b, g, beta):
    B, L, _ = h.shape
    full = lambda shape: pl.BlockSpec(shape, lambda b_, i: (0,) * len(shape))
    return pl.pallas_call(
        _conv_kernel,
        grid=(B, L // TC),
        in_specs=[
            pl.BlockSpec((None, TC, D_CONV), lambda b_, i: (b_, i, 0)),
            pl.BlockSpec((None, HALO, D_CONV),
                         lambda b_, i: (b_, jnp.maximum(i * (TC // HALO) - 1, 0), 0)),
            full(w.shape), full(b.shape), full(g.shape), full(beta.shape),
        ],
        out_specs=pl.BlockSpec((None, D_CONV, TC), lambda b_, i: (b_, 0, i)),
        out_shape=jax.ShapeDtypeStruct((B, D_CONV, L), BF16),
        scratch_shapes=[pltpu.VMEM((HALO + TC + SUBLANES, D_CONV), F32),
                        pltpu.VMEM((SUBLANES, HALO + TC, D_CONV), F32),
                        pltpu.VMEM((TC, D_CONV), F32)],
        compiler_params=pltpu.CompilerParams(
            dimension_semantics=("parallel", "parallel"), vmem_limit_bytes=VMEM_LIMIT),
        name="conv",
    )(h, h, w, b, g, beta)


def _dsa_kernel(n_sel, qz_ref, qi_ref, wi_ref, k_ref, ki_ref, v_ref, ltri_ref, o_ref,
                keys_ref, hi_ref, lo_ref, acc_ref, m_ref, bias_ref, att_ref, ties_ref):
    qb = pl.program_id(1)
    nc = qb + 1
    rows = lax.broadcasted_iota(I32, (TK, TQ), 0)
    cols = lax.broadcasted_iota(I32, (TK, TQ), 1)

    def score_chunk(c, carry):
        kic = ki_ref[pl.ds(pl.multiple_of(c * TK, TK), TK), :]
        s = jnp.zeros((TK, TQ), F32)
        for h in range(IDX_HEADS):
            z = _dot(kic, qi_ref[h * IDX_DIM:(h + 1) * IDX_DIM, :])
            s = s + wi_ref[h:h + 1, :] * jnp.maximum(z, 0.0)
        s = jnp.where(s == 0.0, 0.0, s)
        bits = pltpu.bitcast(s, I32)
        key = jnp.where(bits < 0, bits ^ jnp.int32(0x7FFFFFFF), bits)
        valid = (c * TK + rows) <= (qb * TQ + cols)
        key = jnp.where(valid, key, jnp.int32(INT_MIN))
        keys_ref[c] = key
        hi_ref[c] = lax.shift_right_arithmetic(key, 16).astype(I16)
        return carry

    lax.fori_loop(0, nc, score_chunk, 0)

    n_pair = (nc + 1) // 2

    @pl.when(nc % 2 == 1)
    def _():
        keys_ref[nc] = jnp.full((TK, TQ), INT_MIN, I32)
        hi_ref[nc] = jnp.full((TK, TQ), I16_MIN, I16)

    def count_ge(cand):
        def body(i, cnt):
            a = jnp.sum((keys_ref[2 * i] >= cand).astype(I32), axis=0, keepdims=True)
            b = jnp.sum((keys_ref[2 * i + 1] >= cand).astype(I32), axis=0, keepdims=True)
            return cnt + (a + b)
        return lax.fori_loop(0, n_pair, body, jnp.zeros((1, TQ), I32))

    def count_ge16(ref, cand):
        c16 = cand.astype(I16)
        one, none = jnp.ones((), I16), jnp.zeros((), I16)

        def col_count(x):
            hits = jnp.where(x >= c16, one, none)
            part = hits[0:PACK]
            for r in range(1, TK // PACK):
                part = part + hits[r * PACK:(r + 1) * PACK]
            return jnp.sum(part.astype(I32), axis=0, keepdims=True)

        def body(i, cnt):
            return cnt + (col_count(ref[2 * i]) + col_count(ref[2 * i + 1]))
        return lax.fori_loop(0, n_pair, body, jnp.zeros((1, TQ), I32))

    def search16(ref, need):
        def bit_step(i, prefix):
            cand = prefix + lax.shift_left(jnp.int32(1), 15 - i)
            return jnp.where(count_ge16(ref, cand) >= need, cand, prefix)
        return lax.fori_loop(0, 16, bit_step, jnp.full((1, TQ), I16_MIN, I32))

    t_hi = search16(hi_ref, n_sel)
    above = jnp.where(t_hi == I16_MAX, 0, count_ge16(hi_ref, jnp.minimum(t_hi + 1, I16_MAX)))

    def low_chunk(c, carry):
        key = keys_ref[c]
        low = (key & jnp.int32(0xFFFF)) + jnp.int32(I16_MIN)
        in_bucket = lax.shift_right_arithmetic(key, 16) == t_hi
        lo_ref[c] = jnp.where(in_bucket, low, jnp.int32(I16_MIN)).astype(I16)
        return carry

    lax.fori_loop(0, 2 * n_pair, low_chunk, 0)
    t_lo = search16(lo_ref, n_sel - above)
    prefix = lax.shift_left(t_hi, 16) + (t_lo - jnp.int32(I16_MIN))
    thr = jnp.maximum(prefix, jnp.int32(INT_MIN + 1))
    take = (n_sel - count_ge(thr + 1)).astype(F32)

    acc_ref[...] = jnp.zeros_like(acc_ref)
    m_ref[...] = jnp.full_like(m_ref, NEG_BIG)
    ties_ref[...] = jnp.zeros_like(ties_ref)

    def attn_chunk(c, carry):
        kk = keys_ref[c]
        tie = kk == thr
        tie_b = jnp.where(tie, 1.0, 0.0).astype(BF16)
        before = _dot(ltri_ref[...], tie_b) + ties_ref[...]
        tie_bias = jnp.where(before < take, 0.0, NEG_BIG)
        bias_ref[...] = jnp.where(kk > thr, 0.0, jnp.where(tie, tie_bias, NEG_BIG))
        ties_ref[...] += jnp.sum(tie_b.astype(F32), axis=0, keepdims=True)
        start = pl.multiple_of(c * TK, TK)

        def logits(h):
            kc = k_ref[pl.ds(start, TK), (h // 2) * 128:(h // 2 + 1) * 128]
            att_ref[h] = _dot(kc, qz_ref[h * 128:(h + 1) * 128, :]) + bias_ref[...]

        for h in range(min(LOOKAHEAD, ATTN_HEADS)):
            logits(h)
        for h in range(ATTN_HEADS):
            if h % 2 == 0:
                for hh in range(h + LOOKAHEAD, min(h + LOOKAHEAD + 2, ATTN_HEADS)):
                    logits(hh)
            m_old = m_ref[h:h + 1, :]
            m_new = jnp.maximum(m_old, jnp.max(att_ref[h], axis=0, keepdims=True))
            p = jnp.exp2(att_ref[h] - m_new)
            alpha = jnp.exp2(m_old - m_new)
            hs = slice(h * VROWS, (h + 1) * VROWS)
            pv = _dot(v_ref[c, hs, :], p.astype(BF16))
            acc_ref[hs, :] = alpha * acc_ref[hs, :] + pv
            m_ref[h:h + 1, :] = m_new
        return carry

    lax.fori_loop(0, nc, attn_chunk, 0)

    for h in range(ATTN_HEADS):
        num = acc_ref[h * VROWS:h * VROWS + HEAD_DIM, :]
        den = acc_ref[h * VROWS + HEAD_DIM:h * VROWS + HEAD_DIM + 1, :]
        o_ref[h * 64:(h + 1) * 64, :] = (num / den).astype(BF16)


def _dsa(qz, qiT, wiT, k, ki, vTc):
    B, _, L = qiT.shape
    n_sel = min(TOPK_MAX, L // 4)
    nck = L // TK
    ltri = jnp.tril(jnp.ones((TK, TK), BF16), -1)
    return pl.pallas_call(
        functools.partial(_dsa_kernel, n_sel),
        grid=(B, L // TQ),
        in_specs=[
            pl.BlockSpec((None, ATTN_HEADS * 128, TQ), lambda b, q: (b, 0, q)),
            pl.BlockSpec((None, D_ATTN, TQ), lambda b, q: (b, 0, q)),
            pl.BlockSpec((None, IDX_HEADS, TQ), lambda b, q: (b, 0, q)),
            pl.BlockSpec((None, L, D_ATTN), lambda b, q: (b, 0, 0)),
            pl.BlockSpec((None, L, IDX_DIM), lambda b, q: (b, 0, 0)),
            pl.BlockSpec((None, nck, ATTN_HEADS * VROWS, TK), lambda b, q: (b, 0, 0, 0)),
            pl.BlockSpec((TK, TK), lambda b, q: (0, 0)),
        ],
        out_specs=pl.BlockSpec((None, D_ATTN, TQ), lambda b, q: (b, 0, q)),
        out_shape=jax.ShapeDtypeStruct((B, D_ATTN, L), BF16),
        scratch_shapes=[
            pltpu.VMEM((nck, TK, TQ), I32),
            pltpu.VMEM((nck, TK, TQ), I16),
            pltpu.VMEM((nck, TK, TQ), I16),
            pltpu.VMEM((ATTN_HEADS * VROWS, TQ), F32),
            pltpu.VMEM((ATTN_HEADS, TQ), F32),
            pltpu.VMEM((TK, TQ), F32),
            pltpu.VMEM((ATTN_HEADS, TK, TQ), F32),
            pltpu.VMEM((1, TQ), F32),
        ],
        compiler_params=pltpu.CompilerParams(
            dimension_semantics=("parallel", "arbitrary"), vmem_limit_bytes=VMEM_LIMIT),
        name="dsa",
    )(qz, qiT, wiT, k, ki, vTc, ltri)


def _ln_rows(y, g, b):
    mu = jnp.mean(y, axis=0, keepdims=True)
    var = jnp.mean(jnp.square(y - mu), axis=0, keepdims=True)
    return (y - mu) * lax.rsqrt(var + LN_EPS) * g + b


def _mix_kernel(a_ref, c_ref, x_ref, wa_ref, wc_ref, g_ref, b_ref, o_ref, ob_ref):
    mix = _dot(wa_ref[...], a_ref[...]) + _dot(wc_ref[...], c_ref[...])
    y = DEEPNORM_ALPHA * x_ref[...].T + mix
    x1 = _ln_rows(y, _lane_tile(g_ref[...], TN), _lane_tile(b_ref[...], TN))
    o_ref[...] = x1
    ob_ref[...] = x1.astype(BF16)


def _mix(attnT, convT, x, woa, woc, g, b):
    B, L, _ = x.shape
    nl = L // TN
    full = lambda shape: pl.BlockSpec(shape, lambda b_, i: (0,) * len(shape))
    return pl.pallas_call(
        _mix_kernel,
        grid=(B, nl),
        in_specs=[
            pl.BlockSpec((None, D_ATTN, TN), lambda b_, i: (b_, 0, i)),
            pl.BlockSpec((None, D_CONV, TN), lambda b_, i: (b_, 0, i)),
            pl.BlockSpec((None, TN, D_MODEL), lambda b_, i: (b_, i, 0)),
            full(woa.shape), full(woc.shape), full(g.shape), full(b.shape),
        ],
        out_specs=[
            pl.BlockSpec((D_MODEL, TN), lambda b_, i: (0, b_ * nl + i)),
            pl.BlockSpec((D_MODEL, TN), lambda b_, i: (0, b_ * nl + i)),
        ],
        out_shape=[
            jax.ShapeDtypeStruct((D_MODEL, B * L), F32),
            jax.ShapeDtypeStruct((D_MODEL, B * L), BF16),
        ],
        compiler_params=pltpu.CompilerParams(
            dimension_semantics=("parallel", "parallel"), vmem_limit_bytes=VMEM_LIMIT),
        name="mix_ln1",
    )(attnT, convT, x, woa, woc, g, b)


def _extract_top(s, n_rows, steps, exact_ties, want_rank):
    T = s.shape[1]
    orig = s
    iota = lax.broadcasted_iota(I32, (n_rows, T), 0).astype(F32)
    rank = jnp.full((n_rows, T), float(steps), F32) if want_rank else None
    vals = []
    for a in range(steps):
        cur = jnp.max(s, axis=0, keepdims=True)
        hit = s == cur
        if exact_ties:
            first = jnp.min(jnp.where(hit, iota, float(n_rows)), axis=0, keepdims=True)
            hit = iota == first
        if want_rank:
            rank = jnp.where(hit, float(a), rank)
        s = jnp.where(hit, -jnp.inf, s)
        vals.append(cur)
    member = s != orig
    removed = jnp.sum(jnp.where(member, 1.0, 0.0), axis=0, keepdims=True)
    return rank, vals, member, removed


def _pack_words(x):
    bits = pltpu.bitcast(x.astype(BF16).astype(F32), I32)
    return bits | lax.shift_right_logical(bits, 16)


def _route_head(h, qT, s1_ref, s2_ref, r2_ref, c_ref, e1_ref, e2_ref, exact_ties):
    neg_inf = jnp.full((8, TR), -jnp.inf, F32)
    sub8 = lax.broadcasted_iota(I32, (8, TR), 0)
    q1 = qT[h * 128:h * 128 + PEER_HALF].astype(BF16)
    q2 = qT[h * 128 + PEER_HALF:(h + 1) * 128].astype(BF16)
    s1 = _dot(s1_ref[h], q1)
    s2 = _dot(s2_ref[h], q2)
    r1, v1, in1, n1 = _extract_top(s1, PEER_NKEYS, PEER_TOPK, exact_ties, want_rank=exact_ties)
    r2, v2, in2, n2 = _extract_top(s2, PEER_NKEYS, PEER_TOPK, exact_ties, want_rank=True)
    v1m = jnp.concatenate(v1, axis=0)
    v2m = jnp.concatenate(v2, axis=0)
    blocks = [v1[0] + v2m[:8], v1[0] + v2m[8:]]
    for a in range(1, 8):
        blocks.append(jnp.where(sub8 < PEER_TOPK // (a + 1), v1[a] + v2m[:8], neg_inf))
    blocks.append(v1m[8:] + v2[0])
    cand = jnp.concatenate(blocks, axis=0)
    _, vc, chosen, n3 = _extract_top(cand, cand.shape[0], PEER_TOPK, exact_ties, want_rank=False)
    top = v1[0] + v2[0]
    z = jnp.sum(jnp.where(chosen, jnp.exp(cand - top), 0.0), axis=0, keepdims=True)
    chosen_f = chosen.astype(F32)
    per_a = [jnp.sum(chosen_f[:16], axis=0, keepdims=True)]
    for a in range(1, 8):
        per_a.append(jnp.sum(chosen_f[8 + 8 * a:16 + 8 * a], axis=0, keepdims=True))
    for a in range(8, PEER_TOPK):
        per_a.append(chosen_f[64 + a:65 + a])
    if exact_ties:
        cnt = jnp.zeros((PEER_NKEYS, TR), F32)
        for a in range(PEER_TOPK):
            cnt = jnp.where(r1 == float(a), per_a[a], cnt)
    else:
        cnt = jnp.where(s1 + v2[0] >= vc[PEER_TOPK - 1], 1.0, 0.0)
        for a in range(8):
            cnt = jnp.where(s1 == v1[a], per_a[a], cnt)
    r2_ref[h] = r2.astype(BF16)
    c_ref[h] = _pack_words(cnt)
    e1_ref[h] = _pack_words(jnp.where(in1, jnp.exp(s1 - v1[0]), 0.0) / z)
    e2_ref[h] = jnp.where(in2, jnp.exp(s2 - v2[0]), 0.0).astype(BF16)
    return jnp.maximum(jnp.maximum(n1, n2), n3)


def _route_kernel(x_ref, wq_ref, s1_ref, s2_ref, r2_ref, c_ref, e1_ref, e2_ref, q_scr):
    q_scr[...] = _dot(wq_ref[...], x_ref[...])
    refs = (s1_ref, s2_ref, r2_ref, c_ref, e1_ref, e2_ref)
    worst = [_route_head(h, q_scr, *refs, exact_ties=False) for h in range(PEER_HEADS)]
    for h in range(PEER_HEADS):
        @pl.when(jnp.max(worst[h]) > float(PEER_TOPK))
        def _():
            _route_head(h, q_scr, *refs, exact_ties=True)


def _route(x1b, wqT, sub1, sub2):
    _, N = x1b.shape
    full = lambda shape: pl.BlockSpec(shape, lambda i: (0,) * len(shape))
    spec = pl.BlockSpec((PEER_HEADS, PEER_NKEYS, TR), lambda i: (0, 0, i))
    i32 = jax.ShapeDtypeStruct((PEER_HEADS, PEER_NKEYS, N), I32)
    b16 = jax.ShapeDtypeStruct((PEER_HEADS, PEER_NKEYS, N), BF16)
    return pl.pallas_call(
        _route_kernel,
        grid=(N // TR,),
        in_specs=[pl.BlockSpec((D_MODEL, TR), lambda i: (0, i)),
                  full(wqT.shape), full(sub1.shape), full(sub2.shape)],
        out_specs=[spec, spec, spec, spec],
        out_shape=[b16, i32, i32, b16],
        scratch_shapes=[pltpu.VMEM((D_MODEL, TR), F32)],
        compiler_params=pltpu.CompilerParams(
            dimension_semantics=("parallel",), vmem_limit_bytes=VMEM_LIMIT),
        name="peer_route",
    )(x1b, wqT, sub1, sub2)


def _peer_kernel(xb_ref, x_ref, r2_ref, c_ref, e1_ref, e2_ref, u_ref, vt_ref, g_ref, b_ref,
                 o_ref, acc_ref, hh_ref, gh_ref):
    c = pl.program_id(1)

    @pl.when(c == 0)
    def _():
        acc_ref[...] = jnp.zeros_like(acc_ref)

    def packed_rows(words):
        return pltpu.bitcast(jnp.broadcast_to(words, (GROWS // 2, TB)), BF16)

    zero = jnp.zeros((), BF16)
    per_piece = PIECE // PEER_NKEYS

    def gates(k):
        for s in range(k * per_piece, (k + 1) * per_piece):
            for rb in range(PEER_NKEYS // GROWS):
                rs = slice(rb * GROWS, (rb + 1) * GROWS)
                g = None
                for h in range(PEER_HEADS):
                    cb = packed_rows(c_ref[h, s:s + 1, :])
                    eb = packed_rows(e1_ref[h, s:s + 1, :])
                    t = jnp.where(r2_ref[h, rs, :] < cb, e2_ref[h, rs, :], zero) * eb
                    g = t if g is None else g + t
                gh_ref[s * PEER_NKEYS + rb * GROWS:s * PEER_NKEYS + (rb + 1) * GROWS, :] = g

    def up(k):
        ps = slice(k * PIECE, (k + 1) * PIECE)
        hh_ref[ps, :] = _dot(u_ref[ps, :], xb_ref[...])

    def activate(k):
        ps = slice(k * PIECE, (k + 1) * PIECE)
        hh = hh_ref[ps, :]
        act = (hh * (1.0 + lax.erf(hh * (2.0 ** 0.5)))).astype(BF16)
        gh_ref[ps, :] = gh_ref[ps, :] * act

    def down(k0, k1):
        ps = slice(k0 * PIECE, k1 * PIECE)
        acc_ref[...] += _dot(vt_ref[:, ps], gh_ref[ps, :])

    n_piece = EC // PIECE
    down_every = max(n_piece // 2, 1)
    up(0)
    for k in range(n_piece):
        if k + 1 < n_piece:
            up(k + 1)
        gates(k)
        activate(k)
        if (k + 1) % down_every == 0:
            down(k + 1 - down_every, k + 1)

    @pl.when(c == pl.num_programs(1) - 1)
    def _():
        y = DEEPNORM_ALPHA * x_ref[...] + acc_ref[...]
        x2 = _ln_rows(y, _lane_tile(g_ref[...], TB), _lane_tile(b_ref[...], TB))
        o_ref[...] = x2.T


def _peer(x1b, x1, r2, cnt, e1, e2, u, vt, g, b):
    _, N = x1.shape
    rows = EC // PEER_NKEYS
    full = lambda shape: pl.BlockSpec(shape, lambda j, c: (0,) * len(shape))
    tok = pl.BlockSpec((PEER_HEADS, PEER_NKEYS, TB), lambda j, c: (0, 0, j))
    row = pl.BlockSpec((PEER_HEADS, rows, TB), lambda j, c: (0, c, j))
    return pl.pallas_call(
        _peer_kernel,
        grid=(N // TB, PEER_NEXPERTS // EC),
        in_specs=[
            pl.BlockSpec((D_MODEL, TB), lambda j, c: (0, j)),
            pl.BlockSpec((D_MODEL, TB), lambda j, c: (0, j)),
            tok, row, row, tok,
            pl.BlockSpec((EC, D_MODEL), lambda j, c: (c, 0)),
            pl.BlockSpec((D_MODEL, EC), lambda j, c: (0, c)),
            full(g.shape), full(b.shape),
        ],
        out_specs=pl.BlockSpec((TB, D_MODEL), lambda j, c: (j, 0)),
        out_shape=jax.ShapeDtypeStruct((N, D_MODEL), F32),
        scratch_shapes=[pltpu.VMEM((D_MODEL, TB), F32),
                        pltpu.VMEM((EC, TB), F32),
                        pltpu.VMEM((EC, TB), BF16)],
        compiler_params=pltpu.CompilerParams(
            dimension_semantics=("parallel", "arbitrary"), vmem_limit_bytes=VMEM_LIMIT),
        name="peer_dense",
    )(x1b, x1, r2, cnt, e1, e2, u, vt, g, b)


def _col_bcast(v, n):
    return jnp.broadcast_to(v.astype(F32)[:, None], (v.shape[0], n))


def kernel(x, w_in, idx_k_ln_g, idx_k_ln_b, conv_dw_w, conv_dw_b, conv_ln_g, conv_ln_b,
           w_out, ln1_g, ln1_b, peer_w_q, peer_sub_keys1, peer_sub_keys2, peer_u, peer_v,
           ln2_g, ln2_b):
    B, L, _ = x.shape
    assert L % TM == 0 and L % TQ == 0 and TQ == TK and (B * L) % TB == 0
    l = 0

    w = w_in[l]
    wT = w.T.astype(BF16)
    wm = jnp.concatenate([wT[OFF_Q:OFF_KI]], axis=0)
    ws = jnp.concatenate([wT[OFF_KI:OFF_CONV],
                          jnp.zeros((LANES - IDX_DIM - IDX_HEADS, D_MODEL), BF16)], axis=0)
    wc = w[:, OFF_CONV:].astype(BF16)
    woT = w_out[l].T.astype(BF16)
    woa, woc = woT[:, :D_ATTN], woT[:, D_ATTN:]
    wqT = peer_w_q[l].T.astype(BF16)
    sub1 = peer_sub_keys1[l].astype(BF16)
    sub2 = peer_sub_keys2[l].astype(BF16)
    u = (0.5 * peer_u[l]).astype(BF16)
    vt = peer_v[l].T.astype(BF16)

    pos = jnp.arange(L, dtype=F32)
    inv_freq = ROPE_THETA ** (-jnp.arange(0, HEAD_DIM, 2, dtype=F32) / HEAD_DIM)
    ang = inv_freq[:, None] * pos[None, :]
    cosT = jnp.concatenate([jnp.cos(ang), jnp.cos(ang)], axis=0)
    sinT = jnp.concatenate([jnp.sin(ang), jnp.sin(ang)], axis=0)

    qz, qiT, vTc, wiT, k, ki, h = _proj(
        x, wm, ws, wc, cosT, sinT, _col_bcast(idx_k_ln_g[l], TM), _col_bcast(idx_k_ln_b[l], TM))

    dw = jnp.concatenate([conv_dw_w[l], jnp.zeros((1, D_CONV), F32)], axis=0)
    convT = _conv(h, dw, conv_dw_b[l][None, :], conv_ln_g[l][None, :], conv_ln_b[l][None, :])

    attnT = _dsa(qz, qiT, wiT, k, ki, vTc)

    x1, x1b = _mix(attnT, convT, x, woa, woc,
                   _col_bcast(ln1_g[l], LANES), _col_bcast(ln1_b[l], LANES))

    r2, cnt, e1, e2 = _route(x1b, wqT, sub1, sub2)
    out = _peer(x1b, x1, r2, cnt, e1, e2, u, vt,
                _col_bcast(ln2_g[l], LANES), _col_bcast(ln2_b[l], LANES))
    return out.reshape(B, L, D_MODEL)
```

```python
import functools

import jax
import jax.numpy as jnp
from jax import lax
from jax.experimental import pallas as pl
from jax.experimental.pallas import tpu as pltpu

F32 = jnp.float32
BF16 = jnp.bfloat16
I32 = jnp.int32
I16 = jnp.int16

D_MODEL = 1024
ATTN_HEADS = 8
HEAD_DIM = 64
D_ATTN = ATTN_HEADS * HEAD_DIM
D_CONV = 512
CONV_WIDTH = 31
IDX_HEADS = 8
IDX_DIM = 64
TOPK_MAX = 256
ROPE_THETA = 10000.0
PEER_HEADS = 8
PEER_NKEYS = 128
PEER_NEXPERTS = PEER_NKEYS * PEER_NKEYS
PEER_HALF = 64
PEER_TOPK = 16
LN_EPS = 1e-5
DEPTH = 1
DEEPNORM_ALPHA = (2.0 * DEPTH) ** 0.25

OFF_Q = 0
OFF_K = OFF_Q + D_ATTN
OFF_V = OFF_K + D_ATTN
OFF_QI = OFF_V + D_ATTN
OFF_KI = OFF_QI + IDX_HEADS * IDX_DIM
OFF_WI = OFF_KI + IDX_DIM
OFF_CONV = OFF_WI + IDX_HEADS

LANES = 128
SUBLANES = 8
PACK = 16
VMEM_LIMIT = 56 * 1024 * 1024

TM = 512
TC = 512
HALO = 32
TQ = 256
TK = 256
VROWS = 80
LOOKAHEAD = 4
TN = 512
TR = 256
TB = 512
EC = 1024
PIECE = 256
GROWS = 128

LOG2E = 1.4426950408889634
Q_SCALE = HEAD_DIM ** -0.5 * LOG2E
NEG_BIG = -1e30
INT_MIN = -2 ** 31
I16_MIN = -2 ** 15
I16_MAX = 2 ** 15 - 1


def _nt_dot(a, b):
    return lax.dot_general(a, b, (((1,), (1,)), ((), ())), preferred_element_type=F32)


def _dot(a, b):
    return jnp.dot(a, b, preferred_element_type=F32)


def _lane_tile(a, n):
    reps = n // LANES
    return a if reps == 1 else jnp.concatenate([a] * reps, axis=1)


def _rope_rows(z, cos, sin):
    x1, x2 = z[:32], z[32:]
    top = x1 * cos[:32] - x2 * sin[:32]
    bot = x2 * cos[32:] + x1 * sin[32:]
    return top, bot


def _proj_kernel(x_ref, wm_ref, ws_ref, wc_ref, cos_ref, sin_ref, kig_ref, kib_ref,
                 qz_ref, qi_ref, v_ref, wi_ref, k_ref, ki_ref, h_ref, kt_scr):
    xb = x_ref[...].astype(BF16)
    cos = cos_ref[...]
    sin = sin_ref[...]
    pT = _nt_dot(wm_ref[...], xb)

    zeros64 = jnp.zeros((HEAD_DIM, TM), BF16)
    for h in range(ATTN_HEADS):
        top, bot = _rope_rows(pT[OFF_Q + h * 64:OFF_Q + (h + 1) * 64], cos, sin)
        qh = (jnp.concatenate([top, bot], axis=0) * Q_SCALE).astype(BF16)
        if h % 2 == 0:
            qz_ref[h * 128:h * 128 + 64, :] = qh
            qz_ref[h * 128 + 64:(h + 1) * 128, :] = zeros64
        else:
            qz_ref[h * 128:h * 128 + 64, :] = zeros64
            qz_ref[h * 128 + 64:(h + 1) * 128, :] = qh
        top, bot = _rope_rows(pT[OFF_K + h * 64:OFF_K + (h + 1) * 64], cos, sin)
        kt_scr[h * 64:h * 64 + 32, :] = top
        kt_scr[h * 64 + 32:(h + 1) * 64, :] = bot
        top, bot = _rope_rows(pT[OFF_QI + h * 64:OFF_QI + (h + 1) * 64], cos, sin)
        qi_ref[h * 64:h * 64 + 32, :] = top.astype(BF16)
        qi_ref[h * 64 + 32:(h + 1) * 64, :] = bot.astype(BF16)
    k_ref[...] = kt_scr[...].T.astype(BF16)

    vT = pT[OFF_V:OFF_V + D_ATTN].astype(BF16)
    pad_rows = VROWS - HEAD_DIM
    ones_row = (lax.broadcasted_iota(I32, (pad_rows, TK), 0) == 0).astype(F32).astype(BF16)
    for j in range(TM // TK):
        for h in range(ATTN_HEADS):
            v_ref[j, h * VROWS:h * VROWS + HEAD_DIM, :] = vT[h * 64:(h + 1) * 64, j * TK:(j + 1) * TK]
            v_ref[j, h * VROWS + HEAD_DIM:(h + 1) * VROWS, :] = ones_row

    sT = _nt_dot(ws_ref[...], xb)
    z = sT[:IDX_DIM]
    mu = jnp.mean(z, axis=0, keepdims=True)
    var = jnp.mean(jnp.square(z - mu), axis=0, keepdims=True)
    y = (z - mu) * lax.rsqrt(var + LN_EPS) * kig_ref[...] + kib_ref[...]
    top, bot = _rope_rows(y, cos, sin)
    kiT = jnp.concatenate([top, bot, jnp.zeros((LANES - IDX_DIM, TM), F32)], axis=0)
    ki_ref[...] = kiT.T[:, :IDX_DIM].astype(BF16)
    wi_ref[...] = sT[IDX_DIM:IDX_DIM + IDX_HEADS] * (IDX_HEADS ** -0.5)

    a = _dot(xb, wc_ref[...])
    h_ref[...] = a[:, :D_CONV] * jax.nn.sigmoid(a[:, D_CONV:])


def _proj(x, wm, ws, wc, cosT, sinT, kig, kib):
    B, L, _ = x.shape
    grid = (B, L // TM)
    full = lambda shape: pl.BlockSpec(shape, lambda b, i: (0,) * len(shape))
    return pl.pallas_call(
        _proj_kernel,
        grid=grid,
        in_specs=[
            pl.BlockSpec((None, TM, D_MODEL), lambda b, i: (b, i, 0)),
            full(wm.shape), full(ws.shape), full(wc.shape),
            pl.BlockSpec((HEAD_DIM, TM), lambda b, i: (0, i)),
            pl.BlockSpec((HEAD_DIM, TM), lambda b, i: (0, i)),
            full(kig.shape), full(kib.shape),
        ],
        out_specs=[
            pl.BlockSpec((None, ATTN_HEADS * 128, TM), lambda b, i: (b, 0, i)),
            pl.BlockSpec((None, D_ATTN, TM), lambda b, i: (b, 0, i)),
            pl.BlockSpec((None, TM // TK, ATTN_HEADS * VROWS, TK), lambda b, i: (b, i, 0, 0)),
            pl.BlockSpec((None, IDX_HEADS, TM), lambda b, i: (b, 0, i)),
            pl.BlockSpec((None, TM, D_ATTN), lambda b, i: (b, i, 0)),
            pl.BlockSpec((None, TM, IDX_DIM), lambda b, i: (b, i, 0)),
            pl.BlockSpec((None, TM, D_CONV), lambda b, i: (b, i, 0)),
        ],
        out_shape=[
            jax.ShapeDtypeStruct((B, ATTN_HEADS * 128, L), BF16),
            jax.ShapeDtypeStruct((B, D_ATTN, L), BF16),
            jax.ShapeDtypeStruct((B, L // TK, ATTN_HEADS * VROWS, TK), BF16),
            jax.ShapeDtypeStruct((B, IDX_HEADS, L), F32),
            jax.ShapeDtypeStruct((B, L, D_ATTN), BF16),
            jax.ShapeDtypeStruct((B, L, IDX_DIM), BF16),
            jax.ShapeDtypeStruct((B, L, D_CONV), F32),
        ],
        scratch_shapes=[pltpu.VMEM((D_ATTN, TM), F32)],
        compiler_params=pltpu.CompilerParams(
            dimension_semantics=("parallel", "parallel"), vmem_limit_bytes=VMEM_LIMIT),
        name="proj",
    )(x, wm, ws, wc, cosT, sinT, kig, kib)


def _conv_kernel(h_ref, halo_ref, w_ref, b_ref, g_ref, beta_ref, o_ref, buf, shifted, acc):
    i = pl.program_id(1)
    rows = HALO + TC
    buf[0:HALO, :] = jnp.where(i > 0, halo_ref[...], 0.0)
    buf[HALO:rows, :] = h_ref[...]
    buf[rows:rows + SUBLANES, :] = jnp.zeros((SUBLANES, D_CONV), F32)
    for r in range(SUBLANES):
        shifted[r] = buf[r:r + rows, :]
    base = HALO - (CONV_WIDTH - 1)
    for c in range(D_CONV // LANES):
        cs = slice(c * LANES, (c + 1) * LANES)
        for rb in range(TC // LANES):
            a = jnp.zeros((LANES, LANES), F32)
            for j in range(CONV_WIDTH):
                phase = (base + j) % SUBLANES
                s = base + j - phase + rb * LANES
                a = a + shifted[phase, s:s + LANES, cs] * w_ref[j:j + 1, cs]
            acc[rb * LANES:(rb + 1) * LANES, cs] = a
    y = acc[...] + b_ref[...]
    mu = jnp.mean(y, axis=-1, keepdims=True)
    var = jnp.mean(jnp.square(y - mu), axis=-1, keepdims=True)
    y = (y - mu) * lax.rsqrt(var + LN_EPS) * g_ref[...] + beta_ref[...]
    y = y * jax.nn.sigmoid(y)
    o_ref[...] = y.T.astype(BF16)


def _conv(h, w, b, g, beta):
    B, L, _ = h.shape
    full = lambda shape: pl.BlockSpec(shape, lambda b_, i: (0,) * len(shape))
    return pl.pallas_call(
        _conv_kernel,
        grid=(B, L // TC),
        in_specs=[
            pl.BlockSpec((None, TC, D_CONV), lambda b_, i: (b_, i, 0)),
            pl.BlockSpec((None, HALO, D_CONV),
                         lambda b_, i: (b_, jnp.maximum(i * (TC // HALO) - 1, 0), 0)),
            full(w.shape), full(b.shape), full(g.shape), full(beta.shape),
        ],
        out_specs=pl.BlockSpec((None, D_CONV, TC), lambda b_, i: (b_, 0, i)),
        out_shape=jax.ShapeDtypeStruct((B, D_CONV, L), BF16),
        scratch_shapes=[pltpu.VMEM((HALO + TC + SUBLANES, D_CONV), F32),
                        pltpu.VMEM((SUBLANES, HALO + TC, D_CONV), F32),
                        pltpu.VMEM((TC, D_CONV), F32)],
        compiler_params=pltpu.CompilerParams(
            dimension_semantics=("parallel", "parallel"), vmem_limit_bytes=VMEM_LIMIT),
        name="conv",
    )(h, h, w, b, g, beta)


def _dsa_kernel(n_sel, qz_ref, qi_ref, wi_ref, k_ref, ki_ref, v_ref, ltri_ref, o_ref,
                keys_ref, hi_ref, lo_ref, acc_ref, m_ref, bias_ref, att_ref, ties_ref):
    qb = pl.program_id(1)
    nc = qb + 1
    rows = lax.broadcasted_iota(I32, (TK, TQ), 0)
    cols = lax.broadcasted_iota(I32, (TK, TQ), 1)

    def score_chunk(c, carry):
        kic = ki_ref[pl.ds(pl.multiple_of(c * TK, TK), TK), :]
        s = jnp.zeros((TK, TQ), F32)
        for h in range(IDX_HEADS):
            z = _dot(kic, qi_ref[h * IDX_DIM:(h + 1) * IDX_DIM, :])
            s = s + wi_ref[h:h + 1, :] * jnp.maximum(z, 0.0)
        bits = pltpu.bitcast(s, I32)
        key = jnp.where(bits < 0, (bits ^ jnp.int32(0x7FFFFFFF)) + 1, bits)
        valid = (c * TK + rows) <= (qb * TQ + cols)
        key = jnp.where(valid, key, jnp.int32(INT_MIN))
        keys_ref[c] = key
        hi_ref[c] = lax.shift_right_arithmetic(key, 16).astype(I16)
        return carry

    lax.fori_loop(0, nc, score_chunk, 0)

    n_pair = (nc + 1) // 2

    @pl.when(nc % 2 == 1)
    def _():
        keys_ref[nc] = jnp.full((TK, TQ), INT_MIN, I32)
        hi_ref[nc] = jnp.full((TK, TQ), I16_MIN, I16)

    def count_ge(cand):
        def body(i, cnt):
            a = jnp.sum((keys_ref[2 * i] >= cand).astype(I32), axis=0, keepdims=True)
            b = jnp.sum((keys_ref[2 * i + 1] >= cand).astype(I32), axis=0, keepdims=True)
            return cnt + (a + b)
        return lax.fori_loop(0, n_pair, body, jnp.zeros((1, TQ), I32))

    def count_ge16(ref, cand):
        c16 = cand.astype(I16)
        one, none = jnp.ones((), I16), jnp.zeros((), I16)

        def slot_count(x, part):
            hits = jnp.where(x >= c16, one, none)
            for r in range(TK // PACK):
                part = part + hits[r * PACK:(r + 1) * PACK]
            return part

        def body(i, part):
            return slot_count(ref[2 * i + 1], slot_count(ref[2 * i], part))
        part = lax.fori_loop(0, n_pair, body, jnp.zeros((PACK, TQ), I16))
        return jnp.sum(part.astype(I32), axis=0, keepdims=True)

    def search16(ref, need):
        def bit_step(i, prefix):
            cand = prefix + lax.shift_left(jnp.int32(1), 15 - i)
            return jnp.where(count_ge16(ref, cand) >= need, cand, prefix)
        return lax.fori_loop(0, 16, bit_step, jnp.full((1, TQ), I16_MIN, I32))

    t_hi = search16(hi_ref, n_sel)
    above = jnp.where(t_hi == I16_MAX, 0, count_ge16(hi_ref, jnp.minimum(t_hi + 1, I16_MAX)))

    def low_chunk(c, carry):
        key = keys_ref[c]
        low = (key & jnp.int32(0xFFFF)) + jnp.int32(I16_MIN)
        in_bucket = lax.shift_right_arithmetic(key, 16) == t_hi
        lo_ref[c] = jnp.where(in_bucket, low, jnp.int32(I16_MIN)).astype(I16)
        return carry

    lax.fori_loop(0, 2 * n_pair, low_chunk, 0)
    t_lo = search16(lo_ref, n_sel - above)
    prefix = lax.shift_left(t_hi, 16) + (t_lo - jnp.int32(I16_MIN))
    thr = jnp.maximum(prefix, jnp.int32(INT_MIN + 1))
    take = (n_sel - count_ge(thr + 1)).astype(F32)

    acc_ref[...] = jnp.zeros_like(acc_ref)
    m_ref[...] = jnp.full_like(m_ref, NEG_BIG)
    ties_ref[...] = jnp.zeros_like(ties_ref)

    def attn_chunk(c, carry):
        kk = keys_ref[c]
        tie = kk == thr
        tie_b = jnp.where(tie, 1.0, 0.0).astype(BF16)
        before = _dot(ltri_ref[...], tie_b) + ties_ref[...]
        tie_bias = jnp.where(before < take, 0.0, NEG_BIG)
        bias_ref[...] = jnp.where(kk > thr, 0.0, jnp.where(tie, tie_bias, NEG_BIG))
        ties_ref[...] += jnp.sum(tie_b.astype(F32), axis=0, keepdims=True)
        start = pl.multiple_of(c * TK, TK)

        def logits(h):
            kc = k_ref[pl.ds(start, TK), (h // 2) * 128:(h // 2 + 1) * 128]
            att_ref[h] = _dot(kc, qz_ref[h * 128:(h + 1) * 128, :]) + bias_ref[...]

        for h in range(min(LOOKAHEAD, ATTN_HEADS)):
            logits(h)
        for h in range(ATTN_HEADS):
            if h % 2 == 0:
                for hh in range(h + LOOKAHEAD, min(h + LOOKAHEAD + 2, ATTN_HEADS)):
                    logits(hh)
            m_old = m_ref[h:h + 1, :]
            m_new = jnp.maximum(m_old, jnp.max(att_ref[h], axis=0, keepdims=True))
            p = jnp.exp2(att_ref[h] - m_new)
            alpha = jnp.exp2(m_old - m_new)
            hs = slice(h * VROWS, (h + 1) * VROWS)
            pv = _dot(v_ref[c, hs, :], p.astype(BF16))
            acc_ref[hs, :] = alpha * acc_ref[hs, :] + pv
            m_ref[h:h + 1, :] = m_new
        return carry

    lax.fori_loop(0, nc, attn_chunk, 0)

    for h in range(ATTN_HEADS):
        num = acc_ref[h * VROWS:h * VROWS + HEAD_DIM, :]
        den = acc_ref[h * VROWS + HEAD_DIM:h * VROWS + HEAD_DIM + 1, :]
        o_ref[h * 64:(h + 1) * 64, :] = (num / den).astype(BF16)


def _dsa(qz, qiT, wiT, k, ki, vTc):
    B, _, L = qiT.shape
    n_sel = min(TOPK_MAX, L // 4)
    nck = L // TK
    ltri = jnp.tril(jnp.ones((TK, TK), BF16), -1)
    return pl.pallas_call(
        functools.partial(_dsa_kernel, n_sel),
        grid=(B, L // TQ),
        in_specs=[
            pl.BlockSpec((None, ATTN_HEADS * 128, TQ), lambda b, q: (b, 0, q)),
            pl.BlockSpec((None, D_ATTN, TQ), lambda b, q: (b, 0, q)),
            pl.BlockSpec((None, IDX_HEADS, TQ), lambda b, q: (b, 0, q)),
            pl.BlockSpec((None, L, D_ATTN), lambda b, q: (b, 0, 0)),
            pl.BlockSpec((None, L, IDX_DIM), lambda b, q: (b, 0, 0)),
            pl.BlockSpec((None, nck, ATTN_HEADS * VROWS, TK), lambda b, q: (b, 0, 0, 0)),
            pl.BlockSpec((TK, TK), lambda b, q: (0, 0)),
        ],
        out_specs=pl.BlockSpec((None, D_ATTN, TQ), lambda b, q: (b, 0, q)),
        out_shape=jax.ShapeDtypeStruct((B, D_ATTN, L), BF16),
        scratch_shapes=[
            pltpu.VMEM((nck, TK, TQ), I32),
            pltpu.VMEM((nck, TK, TQ), I16),
            pltpu.VMEM((nck, TK, TQ), I16),
            pltpu.VMEM((ATTN_HEADS * VROWS, TQ), F32),
            pltpu.VMEM((ATTN_HEADS, TQ), F32),
            pltpu.VMEM((TK, TQ), F32),
            pltpu.VMEM((ATTN_HEADS, TK, TQ), F32),
            pltpu.VMEM((1, TQ), F32),
        ],
        compiler_params=pltpu.CompilerParams(
            dimension_semantics=("parallel", "arbitrary"), vmem_limit_bytes=VMEM_LIMIT),
        name="dsa",
    )(qz, qiT, wiT, k, ki, vTc, ltri)


def _ln_rows(y, g, b):
    mu = jnp.mean(y, axis=0, keepdims=True)
    var = jnp.mean(jnp.square(y - mu), axis=0, keepdims=True)
    return (y - mu) * lax.rsqrt(var + LN_EPS) * g + b


def _mix_kernel(a_ref, c_ref, x_ref, wa_ref, wc_ref, g_ref, b_ref, o_ref, ob_ref):
    mix = _dot(wa_ref[...], a_ref[...]) + _dot(wc_ref[...], c_ref[...])
    y = DEEPNORM_ALPHA * x_ref[...].T + mix
    x1 = _ln_rows(y, _lane_tile(g_ref[...], TN), _lane_tile(b_ref[...], TN))
    o_ref[...] = x1
    ob_ref[...] = x1.astype(BF16)


def _mix(attnT, convT, x, woa, woc, g, b):
    B, L, _ = x.shape
    nl = L // TN
    full = lambda shape: pl.BlockSpec(shape, lambda b_, i: (0,) * len(shape))
    return pl.pallas_call(
        _mix_kernel,
        grid=(B, nl),
        in_specs=[
            pl.BlockSpec((None, D_ATTN, TN), lambda b_, i: (b_, 0, i)),
            pl.BlockSpec((None, D_CONV, TN), lambda b_, i: (b_, 0, i)),
            pl.BlockSpec((None, TN, D_MODEL), lambda b_, i: (b_, i, 0)),
            full(woa.shape), full(woc.shape), full(g.shape), full(b.shape),
        ],
        out_specs=[
            pl.BlockSpec((D_MODEL, TN), lambda b_, i: (0, b_ * nl + i)),
            pl.BlockSpec((D_MODEL, TN), lambda b_, i: (0, b_ * nl + i)),
        ],
        out_shape=[
            jax.ShapeDtypeStruct((D_MODEL, B * L), F32),
            jax.ShapeDtypeStruct((D_MODEL, B * L), BF16),
        ],
        compiler_params=pltpu.CompilerParams(
            dimension_semantics=("parallel", "parallel"), vmem_limit_bytes=VMEM_LIMIT),
        name="mix_ln1",
    )(attnT, convT, x, woa, woc, g, b)


def _extract_top(s, n_rows, steps, exact_ties, want_rank):
    T = s.shape[1]
    orig = s
    iota = lax.broadcasted_iota(I32, (n_rows, T), 0).astype(F32)
    rank = jnp.full((n_rows, T), float(steps), F32) if want_rank else None
    vals = []
    for a in range(steps):
        cur = jnp.max(s, axis=0, keepdims=True)
        hit = s == cur
        if exact_ties:
            first = jnp.min(jnp.where(hit, iota, float(n_rows)), axis=0, keepdims=True)
            hit = iota == first
        if want_rank:
            rank = jnp.where(hit, float(a), rank)
        s = jnp.where(hit, -jnp.inf, s)
        vals.append(cur)
    member = s != orig
    removed = jnp.sum(jnp.where(member, 1.0, 0.0), axis=0, keepdims=True)
    return rank, vals, member, removed


def _pack_words(x):
    bits = pltpu.bitcast(x.astype(BF16).astype(F32), I32)
    return bits | lax.shift_right_logical(bits, 16)


def _route_head(h, qT, s1_ref, s2_ref, r2_ref, c_ref, e1_ref, e2_ref, exact_ties):
    neg_inf = jnp.full((8, TR), -jnp.inf, F32)
    sub8 = lax.broadcasted_iota(I32, (8, TR), 0)
    q1 = qT[h * 128:h * 128 + PEER_HALF].astype(BF16)
    q2 = qT[h * 128 + PEER_HALF:(h + 1) * 128].astype(BF16)
    s1 = _dot(s1_ref[h], q1)
    s2 = _dot(s2_ref[h], q2)
    r1, v1, in1, n1 = _extract_top(s1, PEER_NKEYS, PEER_TOPK, exact_ties, want_rank=exact_ties)
    r2, v2, in2, n2 = _extract_top(s2, PEER_NKEYS, PEER_TOPK, exact_ties, want_rank=True)
    v1m = jnp.concatenate(v1, axis=0)
    v2m = jnp.concatenate(v2, axis=0)
    blocks = [v1[0] + v2m[:8], v1[0] + v2m[8:]]
    for a in range(1, 8):
        blocks.append(jnp.where(sub8 < PEER_TOPK // (a + 1), v1[a] + v2m[:8], neg_inf))
    blocks.append(v1m[8:] + v2[0])
    cand = jnp.concatenate(blocks, axis=0)
    _, vc, chosen, n3 = _extract_top(cand, cand.shape[0], PEER_TOPK, exact_ties, want_rank=False)
    top = v1[0] + v2[0]
    z = jnp.sum(jnp.where(chosen, jnp.exp(cand - top), 0.0), axis=0, keepdims=True)
    chosen_f = chosen.astype(F32)
    per_a = [jnp.sum(chosen_f[:16], axis=0, keepdims=True)]
    for a in range(1, 8):
        per_a.append(jnp.sum(chosen_f[8 + 8 * a:16 + 8 * a], axis=0, keepdims=True))
    for a in range(8, PEER_TOPK):
        per_a.append(chosen_f[64 + a:65 + a])
    if exact_ties:
        cnt = jnp.zeros((PEER_NKEYS, TR), F32)
        for a in range(PEER_TOPK):
            cnt = jnp.where(r1 == float(a), per_a[a], cnt)
    else:
        cnt = jnp.where(s1 + v2[0] >= vc[PEER_TOPK - 1], 1.0, 0.0)
        for a in range(8):
            cnt = jnp.where(s1 == v1[a], per_a[a], cnt)
    r2_ref[h] = r2.astype(BF16)
    c_ref[h] = _pack_words(cnt)
    e1_ref[h] = _pack_words(jnp.where(in1, jnp.exp(s1 - v1[0]), 0.0) / z)
    e2_ref[h] = jnp.where(in2, jnp.exp(s2 - v2[0]), 0.0).astype(BF16)
    return jnp.maximum(jnp.maximum(n1, n2), n3)


def _route_kernel(x_ref, wq_ref, s1_ref, s2_ref, r2_ref, c_ref, e1_ref, e2_ref, q_scr):
    q_scr[...] = _dot(wq_ref[...], x_ref[...])
    refs = (s1_ref, s2_ref, r2_ref, c_ref, e1_ref, e2_ref)
    worst = [_route_head(h, q_scr, *refs, exact_ties=False) for h in range(PEER_HEADS)]
    for h in range(PEER_HEADS):
        @pl.when(jnp.max(worst[h]) > float(PEER_TOPK))
        def _():
            _route_head(h, q_scr, *refs, exact_ties=True)


def _route(x1b, wqT, sub1, sub2):
    _, N = x1b.shape
    full = lambda shape: pl.BlockSpec(shape, lambda i: (0,) * len(shape))
    spec = pl.BlockSpec((PEER_HEADS, PEER_NKEYS, TR), lambda i: (0, 0, i))
    i32 = jax.ShapeDtypeStruct((PEER_HEADS, PEER_NKEYS, N), I32)
    b16 = jax.ShapeDtypeStruct((PEER_HEADS, PEER_NKEYS, N), BF16)
    return pl.pallas_call(
        _route_kernel,
        grid=(N // TR,),
        in_specs=[pl.BlockSpec((D_MODEL, TR), lambda i: (0, i)),
                  full(wqT.shape), full(sub1.shape), full(sub2.shape)],
        out_specs=[spec, spec, spec, spec],
        out_shape=[b16, i32, i32, b16],
        scratch_shapes=[pltpu.VMEM((D_MODEL, TR), F32)],
        compiler_params=pltpu.CompilerParams(
            dimension_semantics=("parallel",), vmem_limit_bytes=VMEM_LIMIT),
        name="peer_route",
    )(x1b, wqT, sub1, sub2)


def _peer_kernel(xb_ref, x_ref, r2_ref, c_ref, e1_ref, e2_ref, u_ref, vt_ref, g_ref, b_ref,
                 o_ref, acc_ref, hh_ref, gh_ref):
    c = pl.program_id(1)

    @pl.when(c == 0)
    def _():
        acc_ref[...] = jnp.zeros_like(acc_ref)

    def packed_rows(words):
        return pltpu.bitcast(jnp.broadcast_to(words, (GROWS // 2, TB)), BF16)

    zero = jnp.zeros((), BF16)
    per_piece = PIECE // PEER_NKEYS

    def gates(k):
        for s in range(k * per_piece, (k + 1) * per_piece):
            for rb in range(PEER_NKEYS // GROWS):
                rs = slice(rb * GROWS, (rb + 1) * GROWS)
                g = None
                for h in range(PEER_HEADS):
                    cb = packed_rows(c_ref[h, s:s + 1, :])
                    eb = packed_rows(e1_ref[h, s:s + 1, :])
                    t = jnp.where(r2_ref[h, rs, :] < cb, e2_ref[h, rs, :], zero) * eb
                    g = t if g is None else g + t
                gh_ref[s * PEER_NKEYS + rb * GROWS:s * PEER_NKEYS + (rb + 1) * GROWS, :] = g

    def up(k):
        ps = slice(k * PIECE, (k + 1) * PIECE)
        hh_ref[ps, :] = _dot(u_ref[ps, :], xb_ref[...])

    def activate(k):
        ps = slice(k * PIECE, (k + 1) * PIECE)
        hh = hh_ref[ps, :]
        act = (hh * (1.0 + lax.erf(hh * (2.0 ** 0.5)))).astype(BF16)
        gh_ref[ps, :] = gh_ref[ps, :] * act

    def down(k0, k1):
        ps = slice(k0 * PIECE, k1 * PIECE)
        acc_ref[...] += _dot(vt_ref[:, ps], gh_ref[ps, :])

    n_piece = EC // PIECE
    down_every = max(n_piece // 2, 1)
    up(0)
    for k in range(n_piece):
        if k + 1 < n_piece:
            up(k + 1)
        gates(k)
        activate(k)
        if (k + 1) % down_every == 0:
            down(k + 1 - down_every, k + 1)

    @pl.when(c == pl.num_programs(1) - 1)
    def _():
        y = DEEPNORM_ALPHA * x_ref[...] + acc_ref[...]
        x2 = _ln_rows(y, _lane_tile(g_ref[...], TB), _lane_tile(b_ref[...], TB))
        o_ref[...] = x2.T


def _peer(x1b, x1, r2, cnt, e1, e2, u, vt, g, b):
    _, N = x1.shape
    rows = EC // PEER_NKEYS
    full = lambda shape: pl.BlockSpec(shape, lambda j, c: (0,) * len(shape))
    tok = pl.BlockSpec((PEER_HEADS, PEER_NKEYS, TB), lambda j, c: (0, 0, j))
    row = pl.BlockSpec((PEER_HEADS, rows, TB), lambda j, c: (0, c, j))
    return pl.pallas_call(
        _peer_kernel,
        grid=(N // TB, PEER_NEXPERTS // EC),
        in_specs=[
            pl.BlockSpec((D_MODEL, TB), lambda j, c: (0, j)),
            pl.BlockSpec((D_MODEL, TB), lambda j, c: (0, j)),
            tok, row, row, tok,
            pl.BlockSpec((EC, D_MODEL), lambda j, c: (c, 0)),
            pl.BlockSpec((D_MODEL, EC), lambda j, c: (0, c)),
            full(g.shape), full(b.shape),
        ],
        out_specs=pl.BlockSpec((TB, D_MODEL), lambda j, c: (j, 0)),
        out_shape=jax.ShapeDtypeStruct((N, D_MODEL), F32),
        scratch_shapes=[pltpu.VMEM((D_MODEL, TB), F32),
                        pltpu.VMEM((EC, TB), F32),
                        pltpu.VMEM((EC, TB), BF16)],
        compiler_params=pltpu.CompilerParams(
            dimension_semantics=("parallel", "arbitrary"), vmem_limit_bytes=VMEM_LIMIT),
        name="peer_dense",
    )(x1b, x1, r2, cnt, e1, e2, u, vt, g, b)


def _col_bcast(v, n):
    return jnp.broadcast_to(v.astype(F32)[:, None], (v.shape[0], n))


def kernel(x, w_in, idx_k_ln_g, idx_k_ln_b, conv_dw_w, conv_dw_b, conv_ln_g, conv_ln_b,
           w_out, ln1_g, ln1_b, peer_w_q, peer_sub_keys1, peer_sub_keys2, peer_u, peer_v,
           ln2_g, ln2_b):
    B, L, _ = x.shape
    assert L % TM == 0 and L % TQ == 0 and TQ == TK and (B * L) % TB == 0
    l = 0

    w = w_in[l]
    wT = w.T.astype(BF16)
    wm = jnp.concatenate([wT[OFF_Q:OFF_KI]], axis=0)
    ws = jnp.concatenate([wT[OFF_KI:OFF_CONV],
                          jnp.zeros((LANES - IDX_DIM - IDX_HEADS, D_MODEL), BF16)], axis=0)
    wc = w[:, OFF_CONV:].astype(BF16)
    woT = w_out[l].T.astype(BF16)
    woa, woc = woT[:, :D_ATTN], woT[:, D_ATTN:]
    wqT = peer_w_q[l].T.astype(BF16)
    sub1 = peer_sub_keys1[l].astype(BF16)
    sub2 = peer_sub_keys2[l].astype(BF16)
    u = (0.5 * peer_u[l]).astype(BF16)
    vt = peer_v[l].T.astype(BF16)

    pos = jnp.arange(L, dtype=F32)
    inv_freq = ROPE_THETA ** (-jnp.arange(0, HEAD_DIM, 2, dtype=F32) / HEAD_DIM)
    ang = inv_freq[:, None] * pos[None, :]
    cosT = jnp.concatenate([jnp.cos(ang), jnp.cos(ang)], axis=0)
    sinT = jnp.concatenate([jnp.sin(ang), jnp.sin(ang)], axis=0)

    qz, qiT, vTc, wiT, k, ki, h = _proj(
        x, wm, ws, wc, cosT, sinT, _col_bcast(idx_k_ln_g[l], TM), _col_bcast(idx_k_ln_b[l], TM))

    dw = jnp.concatenate([conv_dw_w[l], jnp.zeros((1, D_CONV), F32)], axis=0)
    convT = _conv(h, dw, conv_dw_b[l][None, :], conv_ln_g[l][None, :], conv_ln_b[l][None, :])

    attnT = _dsa(qz, qiT, wiT, k, ki, vTc)

    x1, x1b = _mix(attnT, convT, x, woa, woc,
                   _col_bcast(ln1_g[l], LANES), _col_bcast(ln1_b[l], LANES))

    r2, cnt, e1, e2 = _route(x1b, wqT, sub1, sub2)
    out = _peer(x1b, x1, r2, cnt, e1, e2, u, vt,
                _col_bcast(ln2_g[l], LANES), _col_bcast(ln2_b[l], LANES))
    return out.reshape(B, L, D_MODEL)
```

```python
import functools

import jax
import jax.numpy as jnp
from jax import lax
from jax.experimental import pallas as pl
from jax.experimental.pallas import tpu as pltpu

F32 = jnp.float32
BF16 = jnp.bfloat16
I32 = jnp.int32

D_MODEL = 1024
ATTN_HEADS = 8
HEAD_DIM = 64
D_ATTN = ATTN_HEADS * HEAD_DIM
D_CONV = 512
CONV_WIDTH = 31
IDX_HEADS = 8
IDX_DIM = 64
TOPK_MAX = 256
ROPE_THETA = 10000.0
PEER_HEADS = 8
PEER_NKEYS = 128
PEER_NEXPERTS = PEER_NKEYS * PEER_NKEYS
PEER_HALF = 64
PEER_TOPK = 16
LN_EPS = 1e-5
DEPTH = 1
DEEPNORM_ALPHA = (2.0 * DEPTH) ** 0.25

OFF_Q = 0
OFF_K = OFF_Q + D_ATTN
OFF_V = OFF_K + D_ATTN
OFF_QI = OFF_V + D_ATTN
OFF_KI = OFF_QI + IDX_HEADS * IDX_DIM
OFF_WI = OFF_KI + IDX_DIM
OFF_CONV = OFF_WI + IDX_HEADS

LANES = 128
SUBLANES = 8
PACK = 16
VMEM_LIMIT = 56 * 1024 * 1024

TM = 512
TC = 512
HALO = 32
TQ = 256
TK = 256
VROWS = 80
LOOKAHEAD = 4
TN = 512
TR = 256
TB = 512
EC = 1024
PIECE = 256
GROWS = 128

LOG2E = 1.4426950408889634
Q_SCALE = HEAD_DIM ** -0.5 * LOG2E
NEG_BIG = -1e30
INT_MIN = -2 ** 31
CODE_NEG_INF = -0x7F800000


def _nt_dot(a, b):
    return lax.dot_general(a, b, (((1,), (1,)), ((), ())), preferred_element_type=F32)


def _dot(a, b):
    return jnp.dot(a, b, preferred_element_type=F32)


def _tree(parts, op):
    parts = list(parts)
    while len(parts) > 1:
        nxt = [op(a, b) for a, b in zip(parts[::2], parts[1::2])]
        if len(parts) % 2:
            nxt.append(parts[-1])
        parts = nxt
    return parts[0]


def _lane_tile(a, n):
    reps = n // LANES
    return a if reps == 1 else jnp.concatenate([a] * reps, axis=1)


def _rope_rows(z, cos, sin):
    x1, x2 = z[:32], z[32:]
    top = x1 * cos[:32] - x2 * sin[:32]
    bot = x2 * cos[32:] + x1 * sin[32:]
    return top, bot


def _proj_kernel(x_ref, wm_ref, ws_ref, wc_ref, cos_ref, sin_ref, kig_ref, kib_ref,
                 qz_ref, qi_ref, v_ref, wi_ref, k_ref, ki_ref, h_ref, kt_scr):
    xb = x_ref[...].astype(BF16)
    cos = cos_ref[...]
    sin = sin_ref[...]
    pT = _nt_dot(wm_ref[...], xb)

    zeros64 = jnp.zeros((HEAD_DIM, TM), BF16)
    for h in range(ATTN_HEADS):
        top, bot = _rope_rows(pT[OFF_Q + h * 64:OFF_Q + (h + 1) * 64], cos, sin)
        qh = (jnp.concatenate([top, bot], axis=0) * Q_SCALE).astype(BF16)
        if h % 2 == 0:
            qz_ref[h * 128:h * 128 + 64, :] = qh
            qz_ref[h * 128 + 64:(h + 1) * 128, :] = zeros64
        else:
            qz_ref[h * 128:h * 128 + 64, :] = zeros64
            qz_ref[h * 128 + 64:(h + 1) * 128, :] = qh
        top, bot = _rope_rows(pT[OFF_K + h * 64:OFF_K + (h + 1) * 64], cos, sin)
        kt_scr[h * 64:h * 64 + 32, :] = top
        kt_scr[h * 64 + 32:(h + 1) * 64, :] = bot
        top, bot = _rope_rows(pT[OFF_QI + h * 64:OFF_QI + (h + 1) * 64], cos, sin)
        qi_ref[h * 64:h * 64 + 32, :] = top.astype(BF16)
        qi_ref[h * 64 + 32:(h + 1) * 64, :] = bot.astype(BF16)
    k_ref[...] = kt_scr[...].T.astype(BF16)

    vT = pT[OFF_V:OFF_V + D_ATTN].astype(BF16)
    pad_rows = VROWS - HEAD_DIM
    ones_row = (lax.broadcasted_iota(I32, (pad_rows, TK), 0) == 0).astype(F32).astype(BF16)
    for j in range(TM // TK):
        for h in range(ATTN_HEADS):
            v_ref[j, h * VROWS:h * VROWS + HEAD_DIM, :] = vT[h * 64:(h + 1) * 64, j * TK:(j + 1) * TK]
            v_ref[j, h * VROWS + HEAD_DIM:(h + 1) * VROWS, :] = ones_row

    sT = _nt_dot(ws_ref[...], xb)
    z = sT[:IDX_DIM]
    mu = jnp.mean(z, axis=0, keepdims=True)
    var = jnp.mean(jnp.square(z - mu), axis=0, keepdims=True)
    y = (z - mu) * lax.rsqrt(var + LN_EPS) * kig_ref[...] + kib_ref[...]
    top, bot = _rope_rows(y, cos, sin)
    kiT = jnp.concatenate([top, bot, jnp.zeros((LANES - IDX_DIM, TM), F32)], axis=0)
    ki_ref[...] = kiT.T[:, :IDX_DIM].astype(BF16)
    wi_ref[...] = sT[IDX_DIM:IDX_DIM + IDX_HEADS] * (IDX_HEADS ** -0.5)

    a = _dot(xb, wc_ref[...])
    h_ref[...] = a[:, :D_CONV] * jax.nn.sigmoid(a[:, D_CONV:])


def _proj(x, wm, ws, wc, cosT, sinT, kig, kib):
    B, L, _ = x.shape
    grid = (B, L // TM)
    full = lambda shape: pl.BlockSpec(shape, lambda b, i: (0,) * len(shape))
    return pl.pallas_call(
        _proj_kernel,
        grid=grid,
        in_specs=[
            pl.BlockSpec((None, TM, D_MODEL), lambda b, i: (b, i, 0)),
            full(wm.shape), full(ws.shape), full(wc.shape),
            pl.BlockSpec((HEAD_DIM, TM), lambda b, i: (0, i)),
            pl.BlockSpec((HEAD_DIM, TM), lambda b, i: (0, i)),
            full(kig.shape), full(kib.shape),
        ],
        out_specs=[
            pl.BlockSpec((None, ATTN_HEADS * 128, TM), lambda b, i: (b, 0, i)),
            pl.BlockSpec((None, D_ATTN, TM), lambda b, i: (b, 0, i)),
            pl.BlockSpec((None, TM // TK, ATTN_HEADS * VROWS, TK), lambda b, i: (b, i, 0, 0)),
            pl.BlockSpec((None, IDX_HEADS, TM), lambda b, i: (b, 0, i)),
            pl.BlockSpec((None, TM, D_ATTN), lambda b, i: (b, i, 0)),
            pl.BlockSpec((None, TM, IDX_DIM), lambda b, i: (b, i, 0)),
            pl.BlockSpec((None, TM, D_CONV), lambda b, i: (b, i, 0)),
        ],
        out_shape=[
            jax.ShapeDtypeStruct((B, ATTN_HEADS * 128, L), BF16),
            jax.ShapeDtypeStruct((B, D_ATTN, L), BF16),
            jax.ShapeDtypeStruct((B, L // TK, ATTN_HEADS * VROWS, TK), BF16),
            jax.ShapeDtypeStruct((B, IDX_HEADS, L), F32),
            jax.ShapeDtypeStruct((B, L, D_ATTN), BF16),
            jax.ShapeDtypeStruct((B, L, IDX_DIM), BF16),
            jax.ShapeDtypeStruct((B, L, D_CONV), F32),
        ],
        scratch_shapes=[pltpu.VMEM((D_ATTN, TM), F32)],
        compiler_params=pltpu.CompilerParams(
            dimension_semantics=("parallel", "parallel"), vmem_limit_bytes=VMEM_LIMIT),
        name="proj",
    )(x, wm, ws, wc, cosT, sinT, kig, kib)


def _conv_kernel(h_ref, halo_ref, w_ref, b_ref, g_ref, beta_ref, o_ref, buf, shifted, acc):
    i = pl.program_id(1)
    rows = HALO + TC
    buf[0:HALO, :] = jnp.where(i > 0, halo_ref[...], 0.0)
    buf[HALO:rows, :] = h_ref[...]
    buf[rows:rows + SUBLANES, :] = jnp.zeros((SUBLANES, D_CONV), F32)
    for r in range(SUBLANES):
        shifted[r] = buf[r:r + rows, :]
    base = HALO - (CONV_WIDTH - 1)
    for c in range(D_CONV // LANES):
        cs = slice(c * LANES, (c + 1) * LANES)
        for rb in range(TC // LANES):
            a = jnp.zeros((LANES, LANES), F32)
            for j in range(CONV_WIDTH):
                phase = (base + j) % SUBLANES
                s = base + j - phase + rb * LANES
                a = a + shifted[phase, s:s + LANES, cs] * w_ref[j:j + 1, cs]
            acc[rb * LANES:(rb + 1) * LANES, cs] = a
    y = acc[...] + b_ref[...]
    mu = jnp.mean(y, axis=-1, keepdims=True)
    var = jnp.mean(jnp.square(y - mu), axis=-1, keepdims=True)
    y = (y - mu) * lax.rsqrt(var + LN_EPS) * g_ref[...] + beta_ref[...]
    y = y * jax.nn.sigmoid(y)
    o_ref[...] = y.T.astype(BF16)


def _conv(h, w, b, g, beta):
    B, L, _ = h.shape
    full = lambda shape: pl.BlockSpec(shape, lambda b_, i: (0,) * len(shape))
    return pl.pallas_call(
        _conv_kernel,
        grid=(B, L // TC),
        in_specs=[
            pl.BlockSpec((None, TC, D_CONV), lambda b_, i: (b_, i, 0)),
            pl.BlockSpec((None, HALO, D_CONV),
                         lambda b_, i: (b_, jnp.maximum(i * (TC // HALO) - 1, 0), 0)),
            full(w.shape), full(b.shape), full(g.shape), full(beta.shape),
        ],
        out_specs=pl.BlockSpec((None, D_CONV, TC), lambda b_, i: (b_, 0, i)),
        out_shape=jax.ShapeDtypeStruct((B, D_CONV, L), BF16),
        scratch_shapes=[pltpu.VMEM((HALO + TC + SUBLANES, D_CONV), F32),
                        pltpu.VMEM((SUBLANES, HALO + TC, D_CONV), F32),
                        pltpu.VMEM((TC, D_CONV), F32)],
        compiler_params=pltpu.CompilerParams(
            dimension_semantics=("parallel", "parallel"), vmem_limit_bytes=VMEM_LIMIT),
        name="conv",
    )(h, h, w, b, g, beta)


def _dsa_kernel(n_sel, qz_ref, qi_ref, wi_ref, k_ref, ki_ref, v_ref, ltri_ref, o_ref,
                sc_ref, acc_ref, m_ref, bias_ref, att_ref, ties_ref):
    qb = pl.program_id(1)
    nc = qb + 1
    rows = lax.broadcasted_iota(I32, (TK, TQ), 0)
    cols = lax.broadcasted_iota(I32, (TK, TQ), 1)

    def score_chunk(c, carry):
        kic = ki_ref[pl.ds(pl.multiple_of(c * TK, TK), TK), :]
        s = jnp.zeros((TK, TQ), F32)
        for h in range(IDX_HEADS):
            z = _dot(kic, qi_ref[h * IDX_DIM:(h + 1) * IDX_DIM, :])
            s = s + wi_ref[h:h + 1, :] * jnp.maximum(z, 0.0)
        valid = (c * TK + rows) <= (qb * TQ + cols)
        sc_ref[c] = jnp.where(valid, s, -jnp.inf)
        return carry

    lax.fori_loop(0, nc, score_chunk, 0)

    n_pair = (nc + 1) // 2

    @pl.when(nc % 2 == 1)
    def _():
        sc_ref[nc] = jnp.full((TK, TQ), -jnp.inf, F32)

    def tree_rows(x, rows, op=jnp.add):
        return _tree([x[r * rows:(r + 1) * rows] for r in range(TK // rows)], op)

    def count(cmp, level):
        def body(i, part):
            a = tree_rows(jnp.where(cmp(sc_ref[2 * i], level), 1, 0), SUBLANES)
            b = tree_rows(jnp.where(cmp(sc_ref[2 * i + 1], level), 1, 0), SUBLANES)
            return part + (a + b)
        part = lax.fori_loop(0, n_pair, body, jnp.zeros((SUBLANES, TQ), I32))
        return jnp.sum(part, axis=0, keepdims=True)

    def level_of(code):
        bits = jnp.where(code >= 0, code, (code - 1) ^ jnp.int32(0x7FFFFFFF))
        return pltpu.bitcast(bits, F32)

    zero = jnp.zeros((1, TQ), I32)
    prefix = jnp.where(count(jnp.greater_equal, level_of(zero)) >= n_sel, zero, jnp.int32(INT_MIN))

    def bit_step(i, prefix):
        cand = prefix + lax.shift_left(jnp.int32(1), 30 - i)
        return jnp.where(count(jnp.greater_equal, level_of(cand)) >= n_sel, cand, prefix)

    prefix = lax.fori_loop(0, 31, bit_step, prefix)
    thr = level_of(jnp.maximum(prefix, jnp.int32(CODE_NEG_INF)))
    take = jnp.where(thr == -jnp.inf, 0, n_sel - count(jnp.greater, thr)).astype(F32)

    acc_ref[...] = jnp.zeros_like(acc_ref)
    m_ref[...] = jnp.full_like(m_ref, NEG_BIG)
    ties_ref[...] = jnp.zeros_like(ties_ref)

    def attn_chunk(c, carry):
        kk = sc_ref[c]
        tie = kk == thr
        tie_b = jnp.where(tie, 1.0, 0.0).astype(BF16)
        before = _dot(ltri_ref[...], tie_b) + ties_ref[...]
        tie_bias = jnp.where(before < take, 0.0, NEG_BIG)
        bias_ref[...] = jnp.where(kk > thr, 0.0, jnp.where(tie, tie_bias, NEG_BIG))
        ties_ref[...] += jnp.sum(tie_b.astype(F32), axis=0, keepdims=True)
        start = pl.multiple_of(c * TK, TK)

        def logits(h):
            kc = k_ref[pl.ds(start, TK), (h // 2) * 128:(h // 2 + 1) * 128]
            att_ref[h] = _dot(kc, qz_ref[h * 128:(h + 1) * 128, :]) + bias_ref[...]

        for h in range(min(LOOKAHEAD, ATTN_HEADS)):
            logits(h)
        for h in range(ATTN_HEADS):
            if h % 2 == 0:
                for hh in range(h + LOOKAHEAD, min(h + LOOKAHEAD + 2, ATTN_HEADS)):
                    logits(hh)
            m_old = m_ref[h:h + 1, :]
            col_max = jnp.max(tree_rows(att_ref[h], SUBLANES, jnp.maximum), axis=0, keepdims=True)
            m_new = jnp.maximum(m_old, col_max)
            p = jnp.exp2(att_ref[h] - m_new)
            alpha = jnp.exp2(m_old - m_new)
            hs = slice(h * VROWS, (h + 1) * VROWS)
            pv = _dot(v_ref[c, hs, :], p.astype(BF16))
            acc_ref[hs, :] = alpha * acc_ref[hs, :] + pv
            m_ref[h:h + 1, :] = m_new
        return carry

    lax.fori_loop(0, nc, attn_chunk, 0)

    for h in range(ATTN_HEADS):
        num = acc_ref[h * VROWS:h * VROWS + HEAD_DIM, :]
        den = acc_ref[h * VROWS + HEAD_DIM:h * VROWS + HEAD_DIM + 1, :]
        o_ref[h * 64:(h + 1) * 64, :] = (num / den).astype(BF16)


def _dsa(qz, qiT, wiT, k, ki, vTc):
    B, _, L = qiT.shape
    n_sel = min(TOPK_MAX, L // 4)
    nck = L // TK
    ltri = jnp.tril(jnp.ones((TK, TK), BF16), -1)
    return pl.pallas_call(
        functools.partial(_dsa_kernel, n_sel),
        grid=(B, L // TQ),
        in_specs=[
            pl.BlockSpec((None, ATTN_HEADS * 128, TQ), lambda b, q: (b, 0, q)),
            pl.BlockSpec((None, D_ATTN, TQ), lambda b, q: (b, 0, q)),
            pl.BlockSpec((None, IDX_HEADS, TQ), lambda b, q: (b, 0, q)),
            pl.BlockSpec((None, L, D_ATTN), lambda b, q: (b, 0, 0)),
            pl.BlockSpec((None, L, IDX_DIM), lambda b, q: (b, 0, 0)),
            pl.BlockSpec((None, nck, ATTN_HEADS * VROWS, TK), lambda b, q: (b, 0, 0, 0)),
            pl.BlockSpec((TK, TK), lambda b, q: (0, 0)),
        ],
        out_specs=pl.BlockSpec((None, D_ATTN, TQ), lambda b, q: (b, 0, q)),
        out_shape=jax.ShapeDtypeStruct((B, D_ATTN, L), BF16),
        scratch_shapes=[
            pltpu.VMEM((nck, TK, TQ), F32),
            pltpu.VMEM((ATTN_HEADS * VROWS, TQ), F32),
            pltpu.VMEM((ATTN_HEADS, TQ), F32),
            pltpu.VMEM((TK, TQ), F32),
            pltpu.VMEM((ATTN_HEADS, TK, TQ), F32),
            pltpu.VMEM((1, TQ), F32),
        ],
        compiler_params=pltpu.CompilerParams(
            dimension_semantics=("parallel", "arbitrary"), vmem_limit_bytes=VMEM_LIMIT),
        name="dsa",
    )(qz, qiT, wiT, k, ki, vTc, ltri)


def _ln_rows(y, g, b):
    mu = jnp.mean(y, axis=0, keepdims=True)
    var = jnp.mean(jnp.square(y - mu), axis=0, keepdims=True)
    return (y - mu) * lax.rsqrt(var + LN_EPS) * g + b


def _mix_kernel(a_ref, c_ref, x_ref, wa_ref, wc_ref, g_ref, b_ref, o_ref, ob_ref):
    mix = _dot(wa_ref[...], a_ref[...]) + _dot(wc_ref[...], c_ref[...])
    y = DEEPNORM_ALPHA * x_ref[...].T + mix
    x1 = _ln_rows(y, _lane_tile(g_ref[...], TN), _lane_tile(b_ref[...], TN))
    o_ref[...] = x1
    ob_ref[...] = x1.astype(BF16)


def _mix(attnT, convT, x, woa, woc, g, b):
    B, L, _ = x.shape
    nl = L // TN
    full = lambda shape: pl.BlockSpec(shape, lambda b_, i: (0,) * len(shape))
    return pl.pallas_call(
        _mix_kernel,
        grid=(B, nl),
        in_specs=[
            pl.BlockSpec((None, D_ATTN, TN), lambda b_, i: (b_, 0, i)),
            pl.BlockSpec((None, D_CONV, TN), lambda b_, i: (b_, 0, i)),
            pl.BlockSpec((None, TN, D_MODEL), lambda b_, i: (b_, i, 0)),
            full(woa.shape), full(woc.shape), full(g.shape), full(b.shape),
        ],
        out_specs=[
            pl.BlockSpec((D_MODEL, TN), lambda b_, i: (0, b_ * nl + i)),
            pl.BlockSpec((D_MODEL, TN), lambda b_, i: (0, b_ * nl + i)),
        ],
        out_shape=[
            jax.ShapeDtypeStruct((D_MODEL, B * L), F32),
            jax.ShapeDtypeStruct((D_MODEL, B * L), BF16),
        ],
        compiler_params=pltpu.CompilerParams(
            dimension_semantics=("parallel", "parallel"), vmem_limit_bytes=VMEM_LIMIT),
        name="mix_ln1",
    )(attnT, convT, x, woa, woc, g, b)


def _extract_top(s, n_rows, steps, exact_ties, want_rank):
    T = s.shape[1]
    orig = s
    iota = lax.broadcasted_iota(I32, (n_rows, T), 0).astype(F32)
    rank = jnp.full((n_rows, T), float(steps), F32) if want_rank else None
    vals = []
    for a in range(steps):
        cur = jnp.max(s, axis=0, keepdims=True)
        hit = s == cur
        if exact_ties:
            first = jnp.min(jnp.where(hit, iota, float(n_rows)), axis=0, keepdims=True)
            hit = iota == first
        if want_rank:
            rank = jnp.where(hit, float(a), rank)
        s = jnp.where(hit, -jnp.inf, s)
        vals.append(cur)
    member = s != orig
    removed = jnp.sum(jnp.where(member, 1.0, 0.0), axis=0, keepdims=True)
    return rank, vals, member, removed


def _route_head(h, qT, s1_ref, s2_ref, r2_ref, c_ref, e1_ref, e2_ref, exact_ties):
    neg_inf = jnp.full((8, TR), -jnp.inf, F32)
    sub8 = lax.broadcasted_iota(I32, (8, TR), 0)
    q1 = qT[h * 128:h * 128 + PEER_HALF].astype(BF16)
    q2 = qT[h * 128 + PEER_HALF:(h + 1) * 128].astype(BF16)
    s1 = _dot(s1_ref[h], q1)
    s2 = _dot(s2_ref[h], q2)
    r1, v1, in1, n1 = _extract_top(s1, PEER_NKEYS, PEER_TOPK, exact_ties, want_rank=exact_ties)
    r2, v2, in2, n2 = _extract_top(s2, PEER_NKEYS, PEER_TOPK, exact_ties, want_rank=True)
    v1m = jnp.concatenate(v1, axis=0)
    v2m = jnp.concatenate(v2, axis=0)
    blocks = [v1[0] + v2m[:8], v1[0] + v2m[8:]]
    for a in range(1, 8):
        blocks.append(jnp.where(sub8 < PEER_TOPK // (a + 1), v1[a] + v2m[:8], neg_inf))
    blocks.append(v1m[8:] + v2[0])
    cand = jnp.concatenate(blocks, axis=0)
    _, vc, chosen, n3 = _extract_top(cand, cand.shape[0], PEER_TOPK, exact_ties, want_rank=False)
    top = v1[0] + v2[0]
    z = jnp.sum(jnp.where(chosen, jnp.exp(cand - top), 0.0), axis=0, keepdims=True)
    chosen_f = chosen.astype(F32)
    per_a = [jnp.sum(chosen_f[:16], axis=0, keepdims=True)]
    for a in range(1, 8):
        per_a.append(jnp.sum(chosen_f[8 + 8 * a:16 + 8 * a], axis=0, keepdims=True))
    for a in range(8, PEER_TOPK):
        per_a.append(chosen_f[64 + a:65 + a])
    if exact_ties:
        cnt = jnp.zeros((PEER_NKEYS, TR), F32)
        for a in range(PEER_TOPK):
            cnt = jnp.where(r1 == float(a), per_a[a], cnt)
    else:
        cnt = jnp.where(s1 + v2[0] >= vc[PEER_TOPK - 1], 1.0, 0.0)
        for a in range(8):
            cnt = jnp.where(s1 == v1[a], per_a[a], cnt)
    r2_ref[h] = r2.astype(BF16)
    c_ref[h] = cnt
    e1_ref[h] = jnp.where(in1, jnp.exp(s1 - v1[0]), 0.0) / z
    e2_ref[h] = jnp.where(in2, jnp.exp(s2 - v2[0]), 0.0).astype(BF16)
    return jnp.maximum(jnp.maximum(n1, n2), n3)


def _route_kernel(x_ref, wq_ref, s1_ref, s2_ref, r2_ref, c_ref, e1_ref, e2_ref, q_scr):
    q_scr[...] = _dot(wq_ref[...], x_ref[...])
    refs = (s1_ref, s2_ref, r2_ref, c_ref, e1_ref, e2_ref)
    worst = [_route_head(h, q_scr, *refs, exact_ties=False) for h in range(PEER_HEADS)]
    for h in range(PEER_HEADS):
        @pl.when(jnp.max(worst[h]) > float(PEER_TOPK))
        def _():
            _route_head(h, q_scr, *refs, exact_ties=True)


def _route(x1b, wqT, sub1, sub2):
    _, N = x1b.shape
    full = lambda shape: pl.BlockSpec(shape, lambda i: (0,) * len(shape))
    spec = pl.BlockSpec((PEER_HEADS, PEER_NKEYS, TR), lambda i: (0, 0, i))
    f32 = jax.ShapeDtypeStruct((PEER_HEADS, PEER_NKEYS, N), F32)
    b16 = jax.ShapeDtypeStruct((PEER_HEADS, PEER_NKEYS, N), BF16)
    return pl.pallas_call(
        _route_kernel,
        grid=(N // TR,),
        in_specs=[pl.BlockSpec((D_MODEL, TR), lambda i: (0, i)),
                  full(wqT.shape), full(sub1.shape), full(sub2.shape)],
        out_specs=[spec, spec, spec, spec],
        out_shape=[b16, f32, f32, b16],
        scratch_shapes=[pltpu.VMEM((D_MODEL, TR), F32)],
        compiler_params=pltpu.CompilerParams(
            dimension_semantics=("parallel",), vmem_limit_bytes=VMEM_LIMIT),
        name="peer_route",
    )(x1b, wqT, sub1, sub2)


def _peer_kernel(xb_ref, x_ref, r2_ref, c_ref, e1_ref, e2_ref, u_ref, vt_ref, g_ref, b_ref,
                 o_ref, acc_ref, hh_ref, gh_ref):
    c = pl.program_id(1)

    @pl.when(c == 0)
    def _():
        acc_ref[...] = jnp.zeros_like(acc_ref)

    def packed_rows(row):
        return jnp.broadcast_to(row, (GROWS, TB)).astype(BF16)

    zero = jnp.zeros((), BF16)
    per_piece = PIECE // PEER_NKEYS

    def gates(k):
        for s in range(k * per_piece, (k + 1) * per_piece):
            for rb in range(PEER_NKEYS // GROWS):
                rs = slice(rb * GROWS, (rb + 1) * GROWS)
                g = None
                for h in range(PEER_HEADS):
                    cb = packed_rows(c_ref[h, s:s + 1, :])
                    eb = packed_rows(e1_ref[h, s:s + 1, :])
                    t = jnp.where(r2_ref[h, rs, :] < cb, e2_ref[h, rs, :], zero) * eb
                    g = t if g is None else g + t
                gh_ref[s * PEER_NKEYS + rb * GROWS:s * PEER_NKEYS + (rb + 1) * GROWS, :] = g

    def up(k):
        ps = slice(k * PIECE, (k + 1) * PIECE)
        hh_ref[ps, :] = _dot(u_ref[ps, :], xb_ref[...])

    def activate(k):
        ps = slice(k * PIECE, (k + 1) * PIECE)
        hh = hh_ref[ps, :]
        act = (hh * (1.0 + lax.erf(hh * (2.0 ** 0.5)))).astype(BF16)
        gh_ref[ps, :] = gh_ref[ps, :] * act

    def down(k0, k1):
        ps = slice(k0 * PIECE, k1 * PIECE)
        acc_ref[...] += _dot(vt_ref[:, ps], gh_ref[ps, :])

    n_piece = EC // PIECE
    down_every = max(n_piece // 2, 1)
    up(0)
    for k in range(n_piece):
        if k + 1 < n_piece:
            up(k + 1)
        gates(k)
        activate(k)
        if (k + 1) % down_every == 0:
            down(k + 1 - down_every, k + 1)

    @pl.when(c == pl.num_programs(1) - 1)
    def _():
        y = DEEPNORM_ALPHA * x_ref[...] + acc_ref[...]
        x2 = _ln_rows(y, _lane_tile(g_ref[...], TB), _lane_tile(b_ref[...], TB))
        o_ref[...] = x2.T


def _peer(x1b, x1, r2, cnt, e1, e2, u, vt, g, b):
    _, N = x1.shape
    rows = EC // PEER_NKEYS
    full = lambda shape: pl.BlockSpec(shape, lambda j, c: (0,) * len(shape))
    tok = pl.BlockSpec((PEER_HEADS, PEER_NKEYS, TB), lambda j, c: (0, 0, j))
    row = pl.BlockSpec((PEER_HEADS, rows, TB), lambda j, c: (0, c, j))
    return pl.pallas_call(
        _peer_kernel,
        grid=(N // TB, PEER_NEXPERTS // EC),
        in_specs=[
            pl.BlockSpec((D_MODEL, TB), lambda j, c: (0, j)),
            pl.BlockSpec((D_MODEL, TB), lambda j, c: (0, j)),
            tok, row, row, tok,
            pl.BlockSpec((EC, D_MODEL), lambda j, c: (c, 0)),
            pl.BlockSpec((D_MODEL, EC), lambda j, c: (0, c)),
            full(g.shape), full(b.shape),
        ],
        out_specs=pl.BlockSpec((TB, D_MODEL), lambda j, c: (j, 0)),
        out_shape=jax.ShapeDtypeStruct((N, D_MODEL), F32),
        scratch_shapes=[pltpu.VMEM((D_MODEL, TB), F32),
                        pltpu.VMEM((EC, TB), F32),
                        pltpu.VMEM((EC, TB), BF16)],
        compiler_params=pltpu.CompilerParams(
            dimension_semantics=("parallel", "arbitrary"), vmem_limit_bytes=VMEM_LIMIT),
        name="peer_dense",
    )(x1b, x1, r2, cnt, e1, e2, u, vt, g, b)


def _col_bcast(v, n):
    return jnp.broadcast_to(v.astype(F32)[:, None], (v.shape[0], n))


def kernel(x, w_in, idx_k_ln_g, idx_k_ln_b, conv_dw_w, conv_dw_b, conv_ln_g, conv_ln_b,
           w_out, ln1_g, ln1_b, peer_w_q, peer_sub_keys1, peer_sub_keys2, peer_u, peer_v,
           ln2_g, ln2_b):
    B, L, _ = x.shape
    assert L % TM == 0 and L % TQ == 0 and TQ == TK and (B * L) % TB == 0
    l = 0

    w = w_in[l]
    wT = w.T.astype(BF16)
    wm = jnp.concatenate([wT[OFF_Q:OFF_KI]], axis=0)
    ws = jnp.concatenate([wT[OFF_KI:OFF_CONV],
                          jnp.zeros((LANES - IDX_DIM - IDX_HEADS, D_MODEL), BF16)], axis=0)
    wc = w[:, OFF_CONV:].astype(BF16)
    woT = w_out[l].T.astype(BF16)
    woa, woc = woT[:, :D_ATTN], woT[:, D_ATTN:]
    wqT = peer_w_q[l].T.astype(BF16)
    sub1 = peer_sub_keys1[l].astype(BF16)
    sub2 = peer_sub_keys2[l].astype(BF16)
    u = (0.5 * peer_u[l]).astype(BF16)
    vt = peer_v[l].T.astype(BF16)

    pos = jnp.arange(L, dtype=F32)
    inv_freq = ROPE_THETA ** (-jnp.arange(0, HEAD_DIM, 2, dtype=F32) / HEAD_DIM)
    ang = inv_freq[:, None] * pos[None, :]
    cosT = jnp.concatenate([jnp.cos(ang), jnp.cos(ang)], axis=0)
    sinT = jnp.concatenate([jnp.sin(ang), jnp.sin(ang)], axis=0)

    qz, qiT, vTc, wiT, k, ki, h = _proj(
        x, wm, ws, wc, cosT, sinT, _col_bcast(idx_k_ln_g[l], TM), _col_bcast(idx_k_ln_b[l], TM))

    dw = jnp.concatenate([conv_dw_w[l], jnp.zeros((1, D_CONV), F32)], axis=0)
    convT = _conv(h, dw, conv_dw_b[l][None, :], conv_ln_g[l][None, :], conv_ln_b[l][None, :])

    attnT = _dsa(qz, qiT, wiT, k, ki, vTc)

    x1, x1b = _mix(attnT, convT, x, woa, woc,
                   _col_bcast(ln1_g[l], LANES), _col_bcast(ln1_b[l], LANES))

    r2, cnt, e1, e2 = _route(x1b, wqT, sub1, sub2)
    out = _peer(x1b, x1, r2, cnt, e1, e2, u, vt,
                _col_bcast(ln2_g[l], LANES), _col_bcast(ln2_b[l], LANES))
    return out.reshape(B, L, D_MODEL)
```
